```python
import jax, jax.numpy as jnp
from jax import lax
import numpy as np

D_MODEL = 2048
BATCH = 2
SEQ = 8192
DEPTH = 1

PLE_DIM = 256

ATTN_HEADS = 16
ATTN_KV_HEADS = 4
ATTN_HEAD_DIM = 64
ATTN_GROUP = ATTN_HEADS // ATTN_KV_HEADS
ATTN_Q_WIDTH = ATTN_HEADS * ATTN_HEAD_DIM
ATTN_KV_WIDTH = ATTN_KV_HEADS * ATTN_HEAD_DIM
WINDOW = 128
ATTN_BLOCK = 128
ROPE_THETA = 10000.0

HGRN_WIDTH = D_MODEL // 2
HGRN_EXPAND = 128
HGRN_HEADS = HGRN_WIDTH // HGRN_EXPAND
HGRN_DK = HGRN_EXPAND
HGRN_DV = HGRN_WIDTH // HGRN_HEADS
HGRN_CHUNK = 64

D_FF = ((8 * D_MODEL + 3 * 256 - 1) // (3 * 256)) * 256

RMS_EPS = 1e-6

IN_SIZES = (ATTN_Q_WIDTH, ATTN_KV_WIDTH, ATTN_KV_WIDTH,
            HGRN_HEADS * HGRN_DK, HGRN_HEADS * HGRN_DK, HGRN_HEADS * HGRN_DV, HGRN_HEADS * HGRN_DV,
            D_MODEL, D_MODEL)
IN_WIDTH = sum(IN_SIZES)

kernel_name = "hybrid_swa_sink_hgrn2_gated_block"


def rms_norm(x, gain):
    xf = x.astype(jnp.float32)
    y = xf * lax.rsqrt(jnp.mean(xf * xf, axis=-1, keepdims=True) + RMS_EPS)
    return (y * gain.astype(jnp.float32)).astype(x.dtype)


def rotary(t, positions):
    half = t.shape[-1] // 2
    inv_freq = ROPE_THETA ** (-jnp.arange(half, dtype=jnp.float32) / half)
    ang = positions.astype(jnp.float32)[..., None] * inv_freq
    cos = jnp.cos(ang)[:, :, None, :]
    sin = jnp.sin(ang)[:, :, None, :]
    t1 = t[..., :half].astype(jnp.float32)
    t2 = t[..., half:].astype(jnp.float32)
    return jnp.concatenate([t1 * cos - t2 * sin, t2 * cos + t1 * sin], axis=-1).astype(t.dtype)


def sliding_window_attention(q, k, v, sinks):
    B, S, _, D = q.shape
    nb = S // ATTN_BLOCK
    qb = q.reshape(B, nb, ATTN_BLOCK, ATTN_KV_HEADS, ATTN_GROUP, D)

    def band(t):
        tp = jnp.pad(t, ((0, 0), (ATTN_BLOCK, 0), (0, 0), (0, 0)))
        tb = tp.reshape(B, nb + 1, ATTN_BLOCK, ATTN_KV_HEADS, D)
        return jnp.concatenate([tb[:, :-1], tb[:, 1:]], axis=2)

    kb, vb = band(k), band(v)
    scores = jnp.einsum('bnqhgd,bnkhd->bnhgqk', qb, kb,
                        preferred_element_type=jnp.float32) * (D ** -0.5)
    qi = jnp.arange(ATTN_BLOCK)[:, None] + ATTN_BLOCK
    ki = jnp.arange(2 * ATTN_BLOCK)[None, :]
    local = (ki <= qi) & (qi - ki < WINDOW)
    not_pad = (jnp.arange(nb) > 0)[:, None, None] | (ki >= ATTN_BLOCK)[None]
    valid = local[None] & not_pad
    scores = jnp.where(valid[None, :, None, None], scores, -jnp.inf)
    sink = jnp.broadcast_to(
        sinks.astype(jnp.float32).reshape(1, 1, ATTN_KV_HEADS, ATTN_GROUP, 1, 1),
        scores.shape[:-1] + (1,))
    probs = jax.nn.softmax(jnp.concatenate([scores, sink], axis=-1), axis=-1)[..., :-1]
    out = jnp.einsum('bnhgqk,bnkhd->bnqhgd', probs.astype(v.dtype), vb)
    return out.reshape(B, S, ATTN_HEADS * D)


def hgrn2_recurrence(q, f_logit, i, lb):
    B, S = q.shape[0], q.shape[1]
    nc = S // HGRN_CHUNK
    lb = lb.astype(jnp.float32)
    z = f_logit.astype(jnp.float32)
    log_f = jnp.log(lb + (1.0 - lb) * jax.nn.sigmoid(z))
    k = (1.0 - lb) * jax.nn.sigmoid(-z)
    qf = jax.nn.silu(q.astype(jnp.float32)) * (HGRN_DK ** -0.5)
    vf = i.astype(jnp.float32)

    def to_chunks(t):
        return t.reshape(B, nc, HGRN_CHUNK, HGRN_HEADS, t.shape[-1]).transpose(1, 0, 3, 2, 4)

    qc, kc, vc = to_chunks(qf), to_chunks(k), to_chunks(vf)
    bc = jnp.cumsum(to_chunks(log_f), axis=3)
    causal = jnp.tril(jnp.ones((HGRN_CHUNK, HGRN_CHUNK), dtype=bool))

    def step(state, inp):
        q_c, k_c, v_c, b_c = inp
        b_last = b_c[:, :, -1:, :]
        o_inter = jnp.einsum('bhtk,bhkv->bhtv', q_c * jnp.exp(b_c), state)
        diff = b_c[:, :, :, None, :] - b_c[:, :, None, :, :]
        decay = jnp.exp(jnp.where(causal[:, :, None], diff, -jnp.inf))
        scores = jnp.einsum('bhtk,bhsk,bhtsk->bhts', q_c, k_c, decay)
        o_intra = jnp.einsum('bhts,bhsv->bhtv', scores, v_c)
        new_state = (jnp.exp(b_last[:, :, 0, :])[..., None] * state
                     + jnp.einsum('bhsk,bhsv->bhkv', k_c * jnp.exp(b_last - b_c), v_c))
        return new_state, o_inter + o_intra

    state0 = jnp.zeros((B, HGRN_HEADS, HGRN_DK, HGRN_DV), jnp.float32)
    _, o = lax.scan(step, state0, (qc, kc, vc, bc))
    return o.transpose(1, 0, 3, 2, 4).reshape(B, S, HGRN_HEADS, HGRN_DV).astype(q.dtype)


def setup_inputs(seed: int = 0) -> dict:
    key = jax.random.key(seed)
    ks = jax.random.split(key, 20)

    def dense(k, fan_in, fan_out):
        return jax.random.normal(k, (DEPTH, fan_in, fan_out), jnp.float32) * fan_in ** -0.5

    def gain(k, n):
        return 1.0 + 0.02 * jax.random.normal(k, (DEPTH, n), jnp.float32)

    return {
        "x": jax.random.normal(ks[0], (BATCH, SEQ, D_MODEL), jnp.float32),
        "p": jax.random.normal(ks[1], (DEPTH, BATCH, SEQ, PLE_DIM), jnp.float32),
        "positions": jnp.broadcast_to(jnp.arange(SEQ, dtype=jnp.int32), (BATCH, SEQ)),
        "g_mix_pre": gain(ks[2], D_MODEL),
        "w_in": dense(ks[3], D_MODEL, IN_WIDTH),
        "attn_sinks": jax.random.normal(ks[4], (DEPTH, ATTN_HEADS), jnp.float32),
        "hgrn_lb_logits": 0.5 * jax.random.normal(ks[5], (DEPTH + 1, HGRN_HEADS * HGRN_DK), jnp.float32),
        "hgrn_gnorm": gain(ks[6], HGRN_DV),
        "w_attn_branch": dense(ks[7], ATTN_Q_WIDTH, D_MODEL),
        "w_hgrn_branch": dense(ks[8], HGRN_HEADS * HGRN_DV, D_MODEL),
        "w_out": dense(ks[9], D_MODEL, D_MODEL),
        "g_mix_post": gain(ks[10], D_MODEL),
        "g_ffn_pre": gain(ks[11], D_MODEL),
        "w_gate_up": dense(ks[12], D_MODEL, 2 * D_FF),
        "w_down": dense(ks[13], D_FF, D_MODEL),
        "g_ffn_post": gain(ks[14], D_MODEL),
        "g_ple_pre": gain(ks[15], D_MODEL),
        "w_ple_gate": dense(ks[16], D_MODEL, D_MODEL),
        "w_ple_proj": dense(ks[17], PLE_DIM, D_MODEL),
        "g_ple_post": gain(ks[18], D_MODEL),
    }


def reference(x, p, positions, g_mix_pre, w_in, attn_sinks, hgrn_lb_logits, hgrn_gnorm,
              w_attn_branch, w_hgrn_branch, w_out, g_mix_post, g_ffn_pre, w_gate_up, w_down,
              g_ffn_post, g_ple_pre, w_ple_gate, w_ple_proj, g_ple_post):
    B, S, _ = x.shape
    split_at = np.cumsum(IN_SIZES)[:-1].tolist()
    lb_all = jnp.cumsum(jax.nn.softmax(hgrn_lb_logits.astype(jnp.float32), axis=0), axis=0)
    for layer in range(DEPTH):
        h = rms_norm(x, g_mix_pre[layer])
        proj = h @ w_in[layer]
        aq, ak, av, hq, hf, hi, hg, gate_a, gate_b = jnp.split(proj, split_at, axis=-1)

        aq = rotary(aq.reshape(B, S, ATTN_HEADS, ATTN_HEAD_DIM), positions)
        ak = rotary(ak.reshape(B, S, ATTN_KV_HEADS, ATTN_HEAD_DIM), positions)
        av = av.reshape(B, S, ATTN_KV_HEADS, ATTN_HEAD_DIM)
        y_attn = sliding_window_attention(aq, ak, av, attn_sinks[layer])

        o = hgrn2_recurrence(hq.reshape(B, S, HGRN_HEADS, HGRN_DK),
                             hf.reshape(B, S, HGRN_HEADS, HGRN_DK),
                             hi.reshape(B, S, HGRN_HEADS, HGRN_DV),
                             lb_all[layer].reshape(HGRN_HEADS, HGRN_DK))
        o = rms_norm(o, hgrn_gnorm[layer]) * jax.nn.silu(hg.reshape(B, S, HGRN_HEADS, HGRN_DV))
        y_hgrn = o.reshape(B, S, HGRN_HEADS * HGRN_DV)

        merged = (jax.nn.sigmoid(gate_a) * (y_attn @ w_attn_branch[layer])
                  + jax.nn.sigmoid(gate_b) * (y_hgrn @ w_hgrn_branch[layer]))
        x = x + rms_norm(merged @ w_out[layer], g_mix_post[layer])

        h = rms_norm(x, g_ffn_pre[layer])
        gate, up = jnp.split(h @ w_gate_up[layer], 2, axis=-1)
        x = x + rms_norm((jax.nn.silu(gate) * up) @ w_down[layer], g_ffn_post[layer])

        ple_gate = jax.nn.sigmoid(rms_norm(x, g_ple_pre[layer]) @ w_ple_gate[layer])
        e = (p[layer].astype(x.dtype) @ w_ple_proj[layer]) * ple_gate
        x = x + rms_norm(e, g_ple_post[layer])
    return x
```

```python
import functools

import numpy as np
import jax
import jax.numpy as jnp
from jax import lax
from jax.experimental import pallas as pl
from jax.experimental.pallas import tpu as pltpu

D_MODEL = 2048
PLE_DIM = 256
ATTN_HEADS = 16
ATTN_KV_HEADS = 4
ATTN_HEAD_DIM = 64
ATTN_GROUP = ATTN_HEADS // ATTN_KV_HEADS
ATTN_Q_WIDTH = ATTN_HEADS * ATTN_HEAD_DIM
ATTN_KV_WIDTH = ATTN_KV_HEADS * ATTN_HEAD_DIM
ATTN_BLOCK = 128
ROPE_THETA = 10000.0
HGRN_HEADS = 8
HGRN_DK = 128
HGRN_DV = 128
HGRN_WIDTH = HGRN_HEADS * HGRN_DK
D_FF = 5632
RMS_EPS = 1e-6
IN_SIZES = (ATTN_Q_WIDTH, ATTN_KV_WIDTH, ATTN_KV_WIDTH,
            HGRN_WIDTH, HGRN_WIDTH, HGRN_WIDTH, HGRN_WIDTH, D_MODEL, D_MODEL)
IN_WIDTH = sum(IN_SIZES)
IN_OFFSETS = tuple(int(v) for v in np.cumsum((0,) + IN_SIZES[:-1]))
GATE_WIDTH = 2 * D_MODEL
PROJ_OFFSETS = tuple((off + GATE_WIDTH) % IN_WIDTH for off in IN_OFFSETS)

LANES = 128
HGRN_CHUNK = 128
HGRN_LEVELS = (1, 2, 4, 8, 16, 32, 64)
VMEM_LIMIT_BYTES = 56 * 1024 * 1024

BF16 = jnp.bfloat16
F32 = jnp.float32


def _rms_scale(v):
    return lax.rsqrt(jnp.mean(v * v, axis=-1, keepdims=True) + RMS_EPS)


def _dot(a, b):
    return jnp.dot(a, b, preferred_element_type=F32)


def _block_index(offset, width):
    assert offset % width == 0, (offset, width)
    return offset // width


def _dot_nt(a, b):
    return lax.dot_general(a, b, (((1,), (1,)), ((), ())), preferred_element_type=F32)


def _inproj_kernel(x_ref, g_ref, w_ref, o_ref, h_ref):
    @pl.when(pl.program_id(1) == 0)
    def _():
        x = x_ref[...]
        h_ref[...] = (x * _rms_scale(x) * g_ref[...]).astype(BF16)

    o_ref[...] = _dot(h_ref[...], w_ref[...])


def _inproj(x2, g, w, tm=1024, tn=512):
    t, d = x2.shape
    n = w.shape[1]
    return pl.pallas_call(
        _inproj_kernel,
        grid=(t // tm, n // tn),
        in_specs=[pl.BlockSpec((tm, d), lambda i, j: (i, 0)),
                  pl.BlockSpec((1, d), lambda i, j: (0, 0)),
                  pl.BlockSpec((d, tn), lambda i, j: (0, j))],
        out_specs=pl.BlockSpec((tm, tn), lambda i, j: (i, j)),
        out_shape=jax.ShapeDtypeStruct((t, n), F32),
        scratch_shapes=[pltpu.VMEM((tm, d), BF16)],
        compiler_params=pltpu.CompilerParams(
            dimension_semantics=("arbitrary", "arbitrary"),
            vmem_limit_bytes=VMEM_LIMIT_BYTES),
        name="in_proj",
    )(x2, g, w)


def _attn_kernel(sink_ref, pos_ref, invf_ref, q_ref, k_ref, v_ref, o_ref, kz_ref, vz_ref):
    blk = ATTN_BLOCK
    n = pl.program_id(1)

    @pl.when(n == 0)
    def _():
        kz_ref[...] = jnp.zeros_like(kz_ref)
        vz_ref[...] = jnp.zeros_like(vz_ref)

    ang = pos_ref[...] * invf_ref[...]
    cos = jnp.cos(ang)
    sin = jnp.sin(ang)
    lane = lax.broadcasted_iota(jnp.int32, (blk, LANES), 1)
    first_half = (lane & (ATTN_HEAD_DIM // 2)) == 0
    sin_signed = jnp.where(first_half, -sin, sin)
    low_head = lane < ATTN_HEAD_DIM

    def rope(t):
        partner = jnp.where(first_half,
                            pltpu.roll(t, LANES - ATTN_HEAD_DIM // 2, 1),
                            pltpu.roll(t, ATTN_HEAD_DIM // 2, 1))
        return t * cos + partner * sin_signed

    for c in range(ATTN_KV_WIDTH // LANES):
        kc = rope(k_ref[:, c * LANES:(c + 1) * LANES])
        vc = v_ref[:, c * LANES:(c + 1) * LANES]
        for e in range(2):
            hk = 2 * c + e
            for src, dst in ((kc, kz_ref), (vc, vz_ref)):
                own = jnp.where(low_head if e == 0 else jnp.logical_not(low_head), src, 0.0)
                other = pltpu.roll(own, ATTN_HEAD_DIM, 1)
                lo, hi = (own, other) if e == 0 else (other, own)
                dst[hk, 1 * blk:2 * blk, :] = lo.astype(BF16)
                dst[hk, 3 * blk:4 * blk, :] = hi.astype(BF16)

    row = lax.broadcasted_iota(jnp.int32, (blk, 2 * blk), 0)
    col = lax.broadcasted_iota(jnp.int32, (blk, 2 * blk), 1)
    valid = (col > row) & (col <= row + blk) & ((n > 0) | (col >= blk))
    scale = ATTN_HEAD_DIM ** -0.5

    for hk in range(ATTN_KV_HEADS):
        kcat = kz_ref[hk]
        vcat = vz_ref[hk]
        for rh in range(2):
            g = 2 * hk + rh
            q = (rope(q_ref[:, g * LANES:(g + 1) * LANES]) * scale).astype(BF16)
            s = _dot_nt(q, kcat)
            ps, inv = [], []
            for ch in range(2):
                sink = sink_ref[2 * g + ch]
                sq = jnp.where(valid, s[:, ch * 2 * blk:(ch + 1) * 2 * blk], -jnp.inf)
                m = jnp.maximum(jnp.max(sq, axis=-1, keepdims=True), sink)
                p = jnp.exp(sq - m)
                denom = jnp.sum(p, axis=-1, keepdims=True) + jnp.exp(sink - m)
                ps.append(p.astype(BF16))
                inv.append(1.0 / denom)
            out = _dot(jnp.concatenate(ps, axis=1), vcat)
            out = out * jnp.where(low_head, inv[0], inv[1])
            o_ref[:, g * LANES:(g + 1) * LANES] = out.astype(o_ref.dtype)

    for hk in range(ATTN_KV_HEADS):
        for ref in (kz_ref, vz_ref):
            ref[hk, 0 * blk:1 * blk, :] = ref[hk, 1 * blk:2 * blk, :]
            ref[hk, 2 * blk:3 * blk, :] = ref[hk, 3 * blk:4 * blk, :]


def _attention(proj, pos_col, inv_freq, sinks, batch, seq):
    blk = ATTN_BLOCK
    nb = seq // blk
    t = batch * seq
    q_blk, k_blk, v_blk = (_block_index(PROJ_OFFSETS[0], ATTN_Q_WIDTH),
                           _block_index(PROJ_OFFSETS[1], ATTN_KV_WIDTH),
                           _block_index(PROJ_OFFSETS[2], ATTN_KV_WIDTH))
    return pl.pallas_call(
        _attn_kernel,
        grid=(batch, nb),
        in_specs=[pl.BlockSpec(memory_space=pltpu.SMEM),
                  pl.BlockSpec((blk, 1), lambda b, n: (b * nb + n, 0)),
                  pl.BlockSpec((1, LANES), lambda b, n: (0, 0)),
                  pl.BlockSpec((blk, ATTN_Q_WIDTH), lambda b, n: (b * nb + n, q_blk)),
                  pl.BlockSpec((blk, ATTN_KV_WIDTH), lambda b, n: (b * nb + n, k_blk)),
                  pl.BlockSpec((blk, ATTN_KV_WIDTH), lambda b, n: (b * nb + n, v_blk))],
        out_specs=pl.BlockSpec((blk, ATTN_Q_WIDTH), lambda b, n: (b * nb + n, 0)),
        out_shape=jax.ShapeDtypeStruct((t, ATTN_Q_WIDTH), BF16),
        scratch_shapes=[pltpu.VMEM((ATTN_KV_HEADS, 4 * blk, LANES), BF16),
                        pltpu.VMEM((ATTN_KV_HEADS, 4 * blk, LANES), BF16)],
        compiler_params=pltpu.CompilerParams(
            dimension_semantics=("arbitrary", "arbitrary")),
        name="swa_attention",
    )(sinks, pos_col, inv_freq, proj, proj, proj)


def _hgrn_range_matrix():
    c = HGRN_CHUNK
    u = np.arange(c)[None, :]
    t = np.arange(c)[:, None]
    blocks = []
    for m in HGRN_LEVELS:
        upper = (t & m) != 0
        start = (t // m) * m
        q_rng = (u >= start) & (u <= t)
        k_rng = (u > t) & (u <= start + m - 1)
        blocks.append(np.where(upper, q_rng, k_rng))
    blocks.append(u <= t)
    blocks.append(u > t)
    return np.concatenate(blocks, axis=0).astype(np.float32)


def _hgrn_kernel(lbl_ref, gn_ref, w_ref, q_ref, f_ref, i_ref, g_ref, o_ref, st_ref, *, layer):
    c = HGRN_CHUNK
    n_chunks = q_ref.shape[0] // c

    @pl.when(pl.program_id(2) == 0)
    def _():
        st_ref[...] = jnp.zeros_like(st_ref)

    lg = lbl_ref[...]
    eg = jnp.exp(lg - jnp.max(lg, axis=0, keepdims=True))
    lb = (jnp.sum(eg[0:layer + 1, :], axis=0, keepdims=True)
          / jnp.sum(eg, axis=0, keepdims=True))
    gn = gn_ref[...]

    ti = lax.broadcasted_iota(jnp.int32, (c, c), 0)
    si = lax.broadcasted_iota(jnp.int32, (c, c), 1)

    def chunk(ci, carry):
        r0 = pl.multiple_of(ci * c, c)
        z = f_ref[pl.ds(r0, c), :]
        qr = q_ref[pl.ds(r0, c), :]
        v = i_ref[pl.ds(r0, c), :]
        gate = g_ref[pl.ds(r0, c), :]

        logf = jnp.log(lb + (1.0 - lb) * jax.nn.sigmoid(z))
        kk = (1.0 - lb) * jax.nn.sigmoid(-z)
        qt = qr * jax.nn.sigmoid(qr) * (HGRN_DK ** -0.5)

        hi = logf.astype(BF16)
        mid = (logf - hi.astype(F32)).astype(BF16)
        xs = _dot(w_ref[...], jnp.concatenate([hi, mid], axis=1))
        xs = xs[:, :HGRN_DK] + xs[:, HGRN_DK:]

        a = jnp.where(ti == si, _dot_nt(qt.astype(BF16), kk.astype(BF16)), 0.0)
        for li, m in enumerate(HGRN_LEVELS):
            e = jnp.exp(xs[li * c:(li + 1) * c])
            upper = (ti & m) != 0
            u = jnp.where(upper, qt, kk) * e
            qm = jnp.where(upper, u, 0.0).astype(BF16)
            km = jnp.where(upper, 0.0, u).astype(BF16)
            same_block = (ti & (-2 * m)) == (si & (-2 * m))
            a = a + jnp.where(same_block, _dot_nt(qm, km), 0.0)

        nl = len(HGRN_LEVELS)
        eb = jnp.exp(xs[nl * c:(nl + 1) * c])
        el = jnp.exp(xs[(nl + 1) * c:(nl + 2) * c])

        st = st_ref[...]
        v16 = v.astype(BF16)
        o = _dot_nt((qt * eb).astype(BF16), st.astype(BF16)) + _dot(a.astype(BF16), v16)
        st_ref[...] = st * eb[c - 1:c, :] + _dot(v.T.astype(BF16), (kk * el).astype(BF16))

        y = o * _rms_scale(o) * gn
        y = y * (gate * jax.nn.sigmoid(gate))
        o_ref[pl.ds(r0, c), :] = y.astype(o_ref.dtype)
        return carry

    lax.fori_loop(0, n_chunks, chunk, 0)


def _hgrn(proj, lb_logits, gnorm, layer, batch, seq, rows=1024):
    t = batch * seq
    steps = seq // rows
    q_blk, f_blk, i_blk, g_blk = (_block_index(PROJ_OFFSETS[s], HGRN_DK) for s in (3, 4, 5, 6))
    wmat = jnp.asarray(_hgrn_range_matrix(), dtype=BF16)
    row_spec = lambda base: pl.BlockSpec(
        (rows, HGRN_DK), lambda b, h, s, base=base: (b * steps + s, base + h))
    return pl.pallas_call(
        functools.partial(_hgrn_kernel, layer=layer),
        grid=(batch, HGRN_HEADS, steps),
        in_specs=[pl.BlockSpec((lb_logits.shape[0], HGRN_DK), lambda b, h, s: (0, h)),
                  pl.BlockSpec((1, HGRN_DV), lambda b, h, s: (0, 0)),
                  pl.BlockSpec(wmat.shape, lambda b, h, s: (0, 0)),
                  row_spec(q_blk), row_spec(f_blk), row_spec(i_blk), row_spec(g_blk)],
        out_specs=pl.BlockSpec((rows, HGRN_DV), lambda b, h, s: (b * steps + s, h)),
        out_shape=jax.ShapeDtypeStruct((t, HGRN_WIDTH), BF16),
        scratch_shapes=[pltpu.VMEM((HGRN_DV, HGRN_DK), F32)],
        compiler_params=pltpu.CompilerParams(
            dimension_semantics=("arbitrary", "arbitrary", "arbitrary")),
        name="hgrn2",
    )(lb_logits, gnorm, wmat, proj, proj, proj, proj)


def _merge_kernel(x_ref, ya_ref, yh_ref, ga_ref, gb_ref, wa_ref, wh_ref, wo_ref, gp_ref, o_ref):
    merged = (jax.nn.sigmoid(ga_ref[...]) * _dot(ya_ref[...], wa_ref[...])
              + jax.nn.sigmoid(gb_ref[...]) * _dot(yh_ref[...], wh_ref[...]))
    m2 = _dot(merged.astype(BF16), wo_ref[...])
    o_ref[...] = x_ref[...] + m2 * _rms_scale(m2) * gp_ref[...]


def _merge(x2, y_attn, y_hgrn, proj, wa, wh, wo, g_post, tm=256):
    t, d = x2.shape
    ga_blk, gb_blk = _block_index(PROJ_OFFSETS[7], d), _block_index(PROJ_OFFSETS[8], d)
    const = lambda shape: pl.BlockSpec(shape, lambda i: (0, 0), pipeline_mode=pl.Buffered(1))
    return pl.pallas_call(
        _merge_kernel,
        grid=(t // tm,),
        in_specs=[pl.BlockSpec((tm, d), lambda i: (i, 0)),
                  pl.BlockSpec((tm, ATTN_Q_WIDTH), lambda i: (i, 0)),
                  pl.BlockSpec((tm, HGRN_WIDTH), lambda i: (i, 0)),
                  pl.BlockSpec((tm, d), lambda i: (i, ga_blk)),
                  pl.BlockSpec((tm, d), lambda i: (i, gb_blk)),
                  const(wa.shape), const(wh.shape), const(wo.shape), const((1, d))],
        out_specs=pl.BlockSpec((tm, d), lambda i: (i, 0)),
        out_shape=jax.ShapeDtypeStruct((t, d), F32),
        compiler_params=pltpu.CompilerParams(
            dimension_semantics=("arbitrary",), vmem_limit_bytes=VMEM_LIMIT_BYTES),
        name="merge_out",
    )(x2, y_attn, y_hgrn, proj, proj, wa, wh, wo, g_post)


def _ffn_kernel(x_ref, gpre_ref, wg_ref, wu_ref, wd_ref, gpost_ref, o_ref, h_ref, acc_ref):
    j = pl.program_id(1)

    @pl.when(j == 0)
    def _():
        x = x_ref[...]
        h_ref[...] = (x * _rms_scale(x) * gpre_ref[...]).astype(BF16)
        acc_ref[...] = jnp.zeros_like(acc_ref)

    h = h_ref[...]
    gate = _dot(h, wg_ref[...])
    up = _dot(h, wu_ref[...])
    act = (gate * jax.nn.sigmoid(gate) * up).astype(BF16)
    acc_ref[...] += _dot(act, wd_ref[...])

    @pl.when(j == pl.num_programs(1) - 1)
    def _():
        y = acc_ref[...]
        o_ref[...] = x_ref[...] + y * _rms_scale(y) * gpost_ref[...]


def _ffn(x1, g_pre, w_gate_up, w_down, g_post, tm=512, tf=512):
    t, d = x1.shape
    nf = D_FF // tf
    return pl.pallas_call(
        _ffn_kernel,
        grid=(t // tm, nf),
        in_specs=[pl.BlockSpec((tm, d), lambda i, j: (i, 0)),
                  pl.BlockSpec((1, d), lambda i, j: (0, 0)),
                  pl.BlockSpec((d, tf), lambda i, j: (0, j)),
                  pl.BlockSpec((d, tf), lambda i, j: (0, j + nf)),
                  pl.BlockSpec((tf, d), lambda i, j: (j, 0)),
                  pl.BlockSpec((1, d), lambda i, j: (0, 0))],
        out_specs=pl.BlockSpec((tm, d), lambda i, j: (i, 0)),
        out_shape=jax.ShapeDtypeStruct((t, d), F32),
        scratch_shapes=[pltpu.VMEM((tm, d), BF16), pltpu.VMEM((tm, d), F32)],
        compiler_params=pltpu.CompilerParams(
            dimension_semantics=("arbitrary", "arbitrary"),
            vmem_limit_bytes=VMEM_LIMIT_BYTES),
        name="swiglu_ffn",
    )(x1, g_pre, w_gate_up, w_gate_up, w_down, g_post)


def _ple_kernel(x_ref, p_ref, gpre_ref, wg_ref, wp_ref, gpost_ref, o_ref):
    x = x_ref[...]
    h = (x * _rms_scale(x) * gpre_ref[...]).astype(BF16)
    gate = jax.nn.sigmoid(_dot(h, wg_ref[...]))
    e = _dot(p_ref[...].astype(BF16), wp_ref[...]) * gate
    o_ref[...] = x + e * _rms_scale(e) * gpost_ref[...]


def _ple(x2, p2, g_pre, w_gate, w_proj, g_post, tm=256):
    t, d = x2.shape
    const = lambda shape: pl.BlockSpec(shape, lambda i: (0, 0), pipeline_mode=pl.Buffered(1))
    return pl.pallas_call(
        _ple_kernel,
        grid=(t // tm,),
        in_specs=[pl.BlockSpec((tm, d), lambda i: (i, 0)),
                  pl.BlockSpec((tm, PLE_DIM), lambda i: (i, 0)),
                  const((1, d)), const(w_gate.shape), const(w_proj.shape), const((1, d))],
        out_specs=pl.BlockSpec((tm, d), lambda i: (i, 0)),
        out_shape=jax.ShapeDtypeStruct((t, d), F32),
        compiler_params=pltpu.CompilerParams(
            dimension_semantics=("arbitrary",), vmem_limit_bytes=VMEM_LIMIT_BYTES),
        name="ple",
    )(x2, p2, g_pre, w_gate, w_proj, g_post)


def kernel(x, p, positions, g_mix_pre, w_in, attn_sinks, hgrn_lb_logits, hgrn_gnorm,
           w_attn_branch, w_hgrn_branch, w_out, g_mix_post, g_ffn_pre, w_gate_up, w_down,
           g_ffn_post, g_ple_pre, w_ple_gate, w_ple_proj, g_ple_post):
    batch, seq, d = x.shape
    t = batch * seq
    depth = w_in.shape[0]
    half = ATTN_HEAD_DIM // 2
    inv_freq = ROPE_THETA ** (-(np.arange(LANES) % half).astype(np.float32) / half)
    inv_freq = jnp.asarray(inv_freq, dtype=F32).reshape(1, LANES)
    pos_col = positions.astype(F32).reshape(t, 1)
    row = lambda v: v.reshape(1, -1)

    xs = x.reshape(t, d)
    for layer in range(depth):
        w_rot = jnp.roll(w_in[layer], GATE_WIDTH, axis=1).astype(BF16)
        proj = _inproj(xs, row(g_mix_pre[layer]), w_rot)
        y_attn = _attention(proj, pos_col, inv_freq, attn_sinks[layer], batch, seq)
        y_hgrn = _hgrn(proj, hgrn_lb_logits, row(hgrn_gnorm[layer]), layer, batch, seq)
        xs = _merge(xs, y_attn, y_hgrn, proj, w_attn_branch[layer].astype(BF16),
                    w_hgrn_branch[layer].astype(BF16), w_out[layer].astype(BF16),
                    row(g_mix_post[layer]))
        xs = _ffn(xs, row(g_ffn_pre[layer]), w_gate_up[layer].astype(BF16),
                  w_down[layer].astype(BF16), row(g_ffn_post[layer]))
        xs = _ple(xs, p[layer].reshape(t, PLE_DIM), row(g_ple_pre[layer]),
                  w_ple_gate[layer].astype(BF16), w_ple_proj[layer].astype(BF16),
                  row(g_ple_post[layer]))
    return xs.reshape(batch, seq, d)
```

```python
import functools
import math

import numpy as np
import jax
import jax.numpy as jnp
from jax import lax
from jax.experimental import pallas as pl
from jax.experimental.pallas import tpu as pltpu

D_MODEL = 2048
PLE_DIM = 256
ATTN_HEADS = 16
ATTN_KV_HEADS = 4
ATTN_HEAD_DIM = 64
ATTN_Q_WIDTH = ATTN_HEADS * ATTN_HEAD_DIM
ATTN_KV_WIDTH = ATTN_KV_HEADS * ATTN_HEAD_DIM
ATTN_BLOCK = 128
ROPE_THETA = 10000.0
HGRN_HEADS = 8
HGRN_DK = 128
HGRN_DV = 128
HGRN_WIDTH = HGRN_HEADS * HGRN_DK
D_FF = 5632
RMS_EPS = 1e-6
IN_SIZES = (ATTN_Q_WIDTH, ATTN_KV_WIDTH, ATTN_KV_WIDTH,
            HGRN_WIDTH, HGRN_WIDTH, HGRN_WIDTH, HGRN_WIDTH, D_MODEL, D_MODEL)
IN_WIDTH = sum(IN_SIZES)
IN_OFFSETS = tuple(int(v) for v in np.cumsum((0,) + IN_SIZES[:-1]))
GATE_WIDTH = 2 * D_MODEL
PROJ_OFFSETS = tuple((off + GATE_WIDTH) % IN_WIDTH for off in IN_OFFSETS)

LANES = 128
HGRN_CHUNK = 128
HGRN_LEVELS = (1, 2, 4, 8, 16, 32, 64)
HGRN_HEADS_PER_STEP = 4
MXU_SUBTILE = 512
VMEM_LIMIT_BYTES = 58 * 1024 * 1024
LOG2E = math.log2(math.e)
MASKED_SCORE = -1e30

BF16 = jnp.bfloat16
F32 = jnp.float32


def _rms_scale(v):
    return lax.rsqrt(jnp.mean(v * v, axis=-1, keepdims=True) + RMS_EPS)


def _dot(a, b):
    return jnp.dot(a, b, preferred_element_type=F32)


def _block_index(offset, width):
    assert offset % width == 0, (offset, width)
    return offset // width


def _dot_nt(a, b):
    return lax.dot_general(a, b, (((1,), (1,)), ((), ())), preferred_element_type=F32)


def _inproj_kernel(x_ref, g_ref, w_ref, o_ref, h_ref):
    @pl.when(pl.program_id(1) == 0)
    def _():
        x = x_ref[...]
        h_ref[...] = (x * _rms_scale(x) * g_ref[...]).astype(BF16)

    for c0 in range(0, o_ref.shape[1], MXU_SUBTILE):
        cols = slice(c0, min(c0 + MXU_SUBTILE, o_ref.shape[1]))
        o_ref[:, cols] = _dot(h_ref[...], w_ref[:, cols]).astype(o_ref.dtype)


def _inproj(x2, g, w, tm=1024, tn=2432):
    t, d = x2.shape
    n = w.shape[1]
    assert t % tm == 0 and n % tn == 0
    return pl.pallas_call(
        _inproj_kernel,
        grid=(t // tm, n // tn),
        in_specs=[pl.BlockSpec((tm, d), lambda i, j: (i, 0)),
                  pl.BlockSpec((1, d), lambda i, j: (0, 0)),
                  pl.BlockSpec((d, tn), lambda i, j: (0, j))],
        out_specs=pl.BlockSpec((tm, tn), lambda i, j: (i, j)),
        out_shape=jax.ShapeDtypeStruct((t, n), BF16),
        scratch_shapes=[pltpu.VMEM((tm, d), BF16)],
        compiler_params=pltpu.CompilerParams(
            dimension_semantics=("arbitrary", "arbitrary"),
            vmem_limit_bytes=VMEM_LIMIT_BYTES),
        name="in_proj",
    )(x2, g, w)


def _attn_kernel(sink_ref, pos_ref, invf_ref, ctile_ref, stile_ref, q_ref, k_ref, v_ref, o_ref,
                 kz_ref, vz_ref):
    blk = ATTN_BLOCK
    n = pl.program_id(1)
    cur = n % 2
    prv = 1 - cur

    @pl.when(n == 0)
    def _():
        kz_ref[1] = jnp.zeros(kz_ref.shape[1:], kz_ref.dtype)
        vz_ref[1] = jnp.zeros(vz_ref.shape[1:], vz_ref.dtype)

    ang_t = invf_ref[...] * pos_ref[0]
    q_scale = LOG2E * ATTN_HEAD_DIM ** -0.5

    def spread(tab_t, tile_ref):
        p1 = tab_t.astype(BF16)
        r1 = tab_t - p1.astype(F32)
        p2 = r1.astype(BF16)
        p3 = (r1 - p2.astype(F32)).astype(BF16)
        stacked = jnp.concatenate([p1, p2, p3, jnp.zeros_like(p1)], axis=0)
        return _dot(stacked.astype(F32).T.astype(BF16), tile_ref[...])

    cos_t = jnp.cos(ang_t)
    sin_t = jnp.sin(ang_t)
    cos = spread(cos_t, ctile_ref)
    sin_signed = spread(sin_t, stile_ref)
    cos_q = spread(cos_t * q_scale, ctile_ref)
    sin_q = spread(sin_t * q_scale, stile_ref)

    lane = lax.broadcasted_iota(jnp.int32, (blk, LANES), 1)
    row = lax.broadcasted_iota(jnp.int32, (blk, LANES), 0)
    first_half = (lane & (ATTN_HEAD_DIM // 2)) == 0
    low_head = lane < ATTN_HEAD_DIM
    from_prev = lane > row

    def rope(t, c, s):
        partner = jnp.where(first_half,
                            pltpu.roll(t, LANES - ATTN_HEAD_DIM // 2, 1),
                            pltpu.roll(t, ATTN_HEAD_DIM // 2, 1))
        return t * c + partner * s

    def place(src, dst, c):
        for e in range(2):
            own = jnp.where(low_head if e == 0 else jnp.logical_not(low_head), src, 0.0)
            other = pltpu.roll(own, ATTN_HEAD_DIM, 1)
            lo, hi = (own, other) if e == 0 else (other, own)
            dst[cur, 2 * c + e, 0:blk, 0:LANES] = lo.astype(BF16)
            dst[cur, 2 * c + e, blk:2 * blk, 0:LANES] = hi.astype(BF16)

    def place_values():
        ones_lo = low_head.astype(BF16)
        for hk in range(ATTN_KV_HEADS):
            vz_ref[cur, hk, 0:blk, LANES:2 * LANES] = ones_lo
            vz_ref[cur, hk, blk:2 * blk, LANES:2 * LANES] = 1 - ones_lo
        for c in range(ATTN_KV_WIDTH // LANES):
            place(v_ref[:, c * LANES:(c + 1) * LANES].astype(F32), vz_ref, c)

    for c in range(ATTN_KV_WIDTH // LANES):
        place(rope(k_ref[:, c * LANES:(c + 1) * LANES].astype(F32), cos, sin_signed), kz_ref, c)

    prev_penalty = jnp.where(n > 0, 0.0, MASKED_SCORE)
    groups = ATTN_Q_WIDTH // LANES

    def scores(g):
        q = rope(q_ref[:, g * LANES:(g + 1) * LANES].astype(F32), cos_q, sin_q).astype(BF16)
        return _dot_nt(q, kz_ref[prv, g // 2]), _dot_nt(q, kz_ref[cur, g // 2])

    def fold_and_max(g, s_prev, s_cur):
        folded = []
        for ch in range(2):
            sink = sink_ref[2 * g + ch] * LOG2E
            cols = slice(ch * blk, (ch + 1) * blk)
            s = jnp.where(from_prev, s_prev[:, cols] + prev_penalty, s_cur[:, cols])
            folded.append((s, jnp.maximum(jnp.max(s, axis=-1, keepdims=True), sink), sink))
        return folded

    def weigh(g, folded):
        ps, sink_terms = [], []
        for s, m, sink in folded:
            ps.append(jnp.exp2(s - m))
            sink_terms.append(jnp.exp2(sink - m))
        p_prev = jnp.concatenate([jnp.where(from_prev, p, 0.0).astype(BF16) for p in ps], axis=1)
        p_cur = jnp.concatenate([jnp.where(from_prev, 0.0, p).astype(BF16) for p in ps], axis=1)
        acc = _dot(p_prev, vz_ref[prv, g // 2]) + _dot(p_cur, vz_ref[cur, g // 2])
        return acc, sink_terms

    def finish(g, acc, sink_terms):
        denom = acc[:, LANES:] + jnp.where(low_head, sink_terms[0], sink_terms[1])
        o_ref[:, g * LANES:(g + 1) * LANES] = (acc[:, :LANES] / denom).astype(o_ref.dtype)

    s_vals, f_vals, w_vals = {}, {}, {}
    for k in range(groups + 3):
        if k < groups:
            s_vals[k] = scores(k)
        if k == 1:
            place_values()
        if 0 <= k - 1 < groups:
            f_vals[k - 1] = fold_and_max(k - 1, *s_vals.pop(k - 1))
        if 0 <= k - 2 < groups:
            w_vals[k - 2] = weigh(k - 2, f_vals.pop(k - 2))
        if 0 <= k - 3 < groups:
            finish(k - 3, *w_vals.pop(k - 3))


def _rope_constants():
    half = ATTN_HEAD_DIM // 2
    inv_freq = ROPE_THETA ** (-np.arange(half, dtype=np.float32) / half)
    k = np.arange(LANES)[:, None]
    lane = np.arange(LANES)[None, :]
    hit = ((k < 3 * half) & (k % half == lane % half)).astype(np.float32)
    sign = np.where(lane % ATTN_HEAD_DIM < half, -1.0, 1.0).astype(np.float32)
    return inv_freq.reshape(half, 1), hit, hit * sign


def _attention(proj, positions, sinks, batch, seq):
    blk = ATTN_BLOCK
    nb = seq // blk
    t = batch * seq
    inv_freq, ctile, stile = _rope_constants()
    pos_rows = positions.astype(F32).reshape(t // blk, 1, blk)
    const = lambda a: pl.BlockSpec(a.shape, lambda b, n: (0, 0))
    q_blk, k_blk, v_blk = (_block_index(PROJ_OFFSETS[0], ATTN_Q_WIDTH),
                           _block_index(PROJ_OFFSETS[1], ATTN_KV_WIDTH),
                           _block_index(PROJ_OFFSETS[2], ATTN_KV_WIDTH))
    k_scratch = pltpu.VMEM((2, ATTN_KV_HEADS, 2 * blk, LANES), BF16)
    v_scratch = pltpu.VMEM((2, ATTN_KV_HEADS, 2 * blk, 2 * LANES), BF16)
    return pl.pallas_call(
        _attn_kernel,
        grid=(batch, nb),
        in_specs=[pl.BlockSpec(memory_space=pltpu.SMEM),
                  pl.BlockSpec((1, 1, blk), lambda b, n: (b * nb + n, 0, 0)),
                  const(inv_freq), const(ctile), const(stile),
                  pl.BlockSpec((blk, ATTN_Q_WIDTH), lambda b, n: (b * nb + n, q_blk)),
                  pl.BlockSpec((blk, ATTN_KV_WIDTH), lambda b, n: (b * nb + n, k_blk)),
                  pl.BlockSpec((blk, ATTN_KV_WIDTH), lambda b, n: (b * nb + n, v_blk))],
        out_specs=pl.BlockSpec((blk, ATTN_Q_WIDTH), lambda b, n: (b * nb + n, 0)),
        out_shape=jax.ShapeDtypeStruct((t, ATTN_Q_WIDTH), BF16),
        scratch_shapes=[k_scratch, v_scratch],
        compiler_params=pltpu.CompilerParams(
            dimension_semantics=("arbitrary", "arbitrary")),
        name="swa_attention",
    )(sinks, pos_rows, jnp.asarray(inv_freq), jnp.asarray(ctile, dtype=BF16),
      jnp.asarray(stile, dtype=BF16), proj, proj, proj)


def _hgrn_constants():
    c = HGRN_CHUNK
    u = np.arange(c)[None, :]
    t = np.arange(c)[:, None]
    ranges, masks = [], []
    for m in HGRN_LEVELS:
        upper = (t & m) != 0
        start = (t // m) * m
        q_rng = (u >= start) & (u <= t)
        k_rng = (u > t) & (u <= start + m - 1)
        ranges.append(np.where(upper, q_rng, k_rng))
        masks.append(upper & ((u & m) == 0) & ((t // (2 * m)) == (u // (2 * m))))
    ranges.append(u <= t)
    ranges.append(u > t)
    masks.append(u == t)
    rng = np.concatenate(ranges, axis=0).astype(np.float32)
    return np.concatenate([rng, rng], axis=1), np.concatenate(masks, axis=0).astype(np.float32)


def _hgrn_kernel(lbl_ref, gn_ref, w_ref, m_ref, q_ref, f_ref, i_ref, g_ref, o_ref, st_ref,
                 *, layer):
    c = HGRN_CHUNK
    nl = len(HGRN_LEVELS)
    heads = q_ref.shape[1] // HGRN_DK
    n_chunks = q_ref.shape[0] // c

    @pl.when(pl.program_id(2) == 0)
    def _():
        st_ref[...] = jnp.zeros_like(st_ref)

    lg = lbl_ref[...]
    eg = jnp.exp(lg - jnp.max(lg, axis=0, keepdims=True))
    lb_all = (jnp.sum(eg[0:layer + 1, :], axis=0, keepdims=True)
              / jnp.sum(eg, axis=0, keepdims=True))
    gn = gn_ref[...]

    head_cols = [slice(h * HGRN_DK, (h + 1) * HGRN_DK) for h in range(heads)]

    def gates(ci):
        rows = slice(ci * c, (ci + 1) * c)
        qts, kks, pieces = [], [], []
        for cols in head_cols:
            lb = lb_all[:, cols]
            z = f_ref[rows, cols].astype(F32)
            qr = q_ref[rows, cols].astype(F32)
            sig = jax.nn.sigmoid(z)
            logf = jnp.log2(lb + (1.0 - lb) * sig)
            kks.append((1.0 - lb) * (1.0 - sig))
            qts.append(qr * jax.nn.sigmoid(qr) * (HGRN_DK ** -0.5))
            hi = logf.astype(BF16)
            pieces.append((hi, (logf - hi.astype(F32)).astype(BF16)))
        split = jnp.concatenate(
            [jnp.concatenate([p[0] for p in pieces], axis=1),
             jnp.concatenate([p[1] for p in pieces], axis=1)], axis=0)
        return qts, kks, _dot(w_ref[...], split)

    def intra(ci, qts, kks, xs):
        parts = []
        for h, cols in enumerate(head_cols):
            qt, kk = qts[h], kks[h]
            a = m_ref[nl * c:(nl + 1) * c, :] * _dot_nt(qt.astype(BF16), kk.astype(BF16))
            for li in range(nl):
                e = jnp.exp2(xs[li * c:(li + 1) * c, cols])
                pair = _dot_nt((qt * e).astype(BF16), (kk * e).astype(BF16))
                a = a + m_ref[li * c:(li + 1) * c, :] * pair
            eb = jnp.exp2(xs[nl * c:(nl + 1) * c, cols])
            el = jnp.exp2(xs[(nl + 1) * c:(nl + 2) * c, cols])
            v = i_ref[ci * c:(ci + 1) * c, cols]
            parts.append(((qt * eb).astype(BF16), _dot(a.astype(BF16), v), eb[c - 1:c, :],
                          _dot(v.astype(F32).T.astype(BF16), (kk * el).astype(BF16))))
        return parts

    def carry_state(ci, parts, states):
        rows = slice(ci * c, (ci + 1) * c)
        new_states = []
        for h, cols in enumerate(head_cols):
            q_decayed, o_intra, chunk_decay, kv = parts[h]
            st = states[h]
            o = _dot_nt(q_decayed, st.astype(BF16)) + o_intra
            new_states.append(st * chunk_decay + kv)
            gate = g_ref[rows, cols].astype(F32)
            y = o * _rms_scale(o) * gn
            y = y * (gate * jax.nn.sigmoid(gate))
            o_ref[rows, cols] = y.astype(o_ref.dtype)
        return new_states

    states = [st_ref[h] for h in range(heads)]
    g_vals, i_vals = {0: gates(0)}, {}
    for k in range(n_chunks + 1):
        if k + 1 < n_chunks:
            g_vals[k + 1] = gates(k + 1)
        if k < n_chunks:
            i_vals[k] = intra(k, *g_vals.pop(k))
        if k >= 1:
            states = carry_state(k - 1, i_vals.pop(k - 1), states)
    for h in range(heads):
        st_ref[h] = states[h]


def _hgrn(proj, lb_logits, gnorm, layer, batch, seq, rows=1024):
    t = batch * seq
    steps = seq // rows
    hp = HGRN_HEADS_PER_STEP
    width = hp * HGRN_DK
    q_blk, f_blk, i_blk, g_blk = (_block_index(PROJ_OFFSETS[s], width) for s in (3, 4, 5, 6))
    rng, masks = _hgrn_constants()
    rng = jnp.asarray(rng, dtype=BF16)
    masks = jnp.asarray(masks, dtype=F32)
    const = lambda a: pl.BlockSpec(a.shape, lambda b, h, s: (0, 0))
    row_spec = lambda base: pl.BlockSpec(
        (rows, width), lambda b, h, s, base=base: (b * steps + s, base + h))
    return pl.pallas_call(
        functools.partial(_hgrn_kernel, layer=layer),
        grid=(batch, HGRN_HEADS // hp, steps),
        in_specs=[pl.BlockSpec((lb_logits.shape[0], width), lambda b, h, s: (0, h)),
                  pl.BlockSpec((1, HGRN_DV), lambda b, h, s: (0, 0)),
                  const(rng), const(masks),
                  row_spec(q_blk), row_spec(f_blk), row_spec(i_blk), row_spec(g_blk)],
        out_specs=pl.BlockSpec((rows, width), lambda b, h, s: (b * steps + s, h)),
        out_shape=jax.ShapeDtypeStruct((t, HGRN_WIDTH), BF16),
        scratch_shapes=[pltpu.VMEM((hp, HGRN_DV, HGRN_DK), F32)],
        compiler_params=pltpu.CompilerParams(
            dimension_semantics=("arbitrary", "arbitrary", "arbitrary")),
        name="hgrn2",
    )(lb_logits, gnorm, rng, masks, proj, proj, proj, proj)


def _merge_kernel(x_ref, ya_ref, yh_ref, ga_ref, gb_ref, wa_ref, wh_ref, wo_ref, gp_ref, o_ref):
    merged = (jax.nn.sigmoid(ga_ref[...].astype(F32)) * _dot(ya_ref[...], wa_ref[...])
              + jax.nn.sigmoid(gb_ref[...].astype(F32)) * _dot(yh_ref[...], wh_ref[...]))
    m2 = _dot(merged.astype(BF16), wo_ref[...])
    o_ref[...] = x_ref[...] + m2 * _rms_scale(m2) * gp_ref[...]


def _merge(x2, y_attn, y_hgrn, proj, wa, wh, wo, g_post, tm=256):
    t, d = x2.shape
    ga_blk, gb_blk = _block_index(PROJ_OFFSETS[7], d), _block_index(PROJ_OFFSETS[8], d)
    const = lambda shape: pl.BlockSpec(shape, lambda i: (0, 0), pipeline_mode=pl.Buffered(1))
    return pl.pallas_call(
        _merge_kernel,
        grid=(t // tm,),
        in_specs=[pl.BlockSpec((tm, d), lambda i: (i, 0)),
                  pl.BlockSpec((tm, ATTN_Q_WIDTH), lambda i: (i, 0)),
                  pl.BlockSpec((tm, HGRN_WIDTH), lambda i: (i, 0)),
                  pl.BlockSpec((tm, d), lambda i: (i, ga_blk)),
                  pl.BlockSpec((tm, d), lambda i: (i, gb_blk)),
                  const(wa.shape), const(wh.shape), const(wo.shape), const((1, d))],
        out_specs=pl.BlockSpec((tm, d), lambda i: (i, 0)),
        out_shape=jax.ShapeDtypeStruct((t, d), F32),
        compiler_params=pltpu.CompilerParams(
            dimension_semantics=("arbitrary",), vmem_limit_bytes=VMEM_LIMIT_BYTES),
        name="merge_out",
    )(x2, y_attn, y_hgrn, proj, proj, wa, wh, wo, g_post)


def _ffn_kernel(x_ref, gpre_ref, wg_ref, wu_ref, wd_ref, gpost_ref, o_ref, h_ref):
    j = pl.program_id(1)

    @pl.when(j == 0)
    def _():
        x = x_ref[...]
        h_ref[...] = (x * _rms_scale(x) * gpre_ref[...]).astype(BF16)
        o_ref[...] = jnp.zeros_like(o_ref)

    h = h_ref[...]
    gate = _dot(h, wg_ref[...])
    up = _dot(h, wu_ref[...])
    act = (gate * jax.nn.sigmoid(gate) * up).astype(BF16)
    for c0 in range(0, o_ref.shape[1], MXU_SUBTILE):
        cols = slice(c0, c0 + MXU_SUBTILE)
        o_ref[:, cols] += _dot(act, wd_ref[:, cols])

    @pl.when(j == pl.num_programs(1) - 1)
    def _():
        y = o_ref[...]
        o_ref[...] = x_ref[...] + y * _rms_scale(y) * gpost_ref[...]


def _ffn(x1, g_pre, w_gate_up, w_down, g_post, tm=512, tf=512):
    t, d = x1.shape
    nf = D_FF // tf
    return pl.pallas_call(
        _ffn_kernel,
        grid=(t // tm, nf),
        in_specs=[pl.BlockSpec((tm, d), lambda i, j: (i, 0)),
                  pl.BlockSpec((1, d), lambda i, j: (0, 0)),
                  pl.BlockSpec((d, tf), lambda i, j: (0, j)),
                  pl.BlockSpec((d, tf), lambda i, j: (0, j + nf)),
                  pl.BlockSpec((tf, d), lambda i, j: (j, 0)),
                  pl.BlockSpec((1, d), lambda i, j: (0, 0))],
        out_specs=pl.BlockSpec((tm, d), lambda i, j: (i, 0)),
        out_shape=jax.ShapeDtypeStruct((t, d), F32),
        scratch_shapes=[pltpu.VMEM((tm, d), BF16)],
        compiler_params=pltpu.CompilerParams(
            dimension_semantics=("arbitrary", "arbitrary"),
            vmem_limit_bytes=VMEM_LIMIT_BYTES),
        name="swiglu_ffn",
    )(x1, g_pre, w_gate_up, w_gate_up, w_down, g_post)


def _ple_kernel(x_ref, p_ref, gpre_ref, wg_ref, wp_ref, gpost_ref, o_ref):
    x = x_ref[...]
    h = (x * _rms_scale(x) * gpre_ref[...]).astype(BF16)
    gate = jax.nn.sigmoid(_dot(h, wg_ref[...]))
    e = _dot(p_ref[...].astype(BF16), wp_ref[...]) * gate
    o_ref[...] = x + e * _rms_scale(e) * gpost_ref[...]


def _ple(x2, p2, g_pre, w_gate, w_proj, g_post, tm=256):
    t, d = x2.shape
    const = lambda shape: pl.BlockSpec(shape, lambda i: (0, 0), pipeline_mode=pl.Buffered(1))
    return pl.pallas_call(
        _ple_kernel,
        grid=(t // tm,),
        in_specs=[pl.BlockSpec((tm, d), lambda i: (i, 0)),
                  pl.BlockSpec((tm, PLE_DIM), lambda i: (i, 0)),
                  const((1, d)), const(w_gate.shape), const(w_proj.shape), const((1, d))],
        out_specs=pl.BlockSpec((tm, d), lambda i: (i, 0)),
        out_shape=jax.ShapeDtypeStruct((t, d), F32),
        compiler_params=pltpu.CompilerParams(
            dimension_semantics=("arbitrary",), vmem_limit_bytes=VMEM_LIMIT_BYTES),
        name="ple",
    )(x2, p2, g_pre, w_gate, w_proj, g_post)


def kernel(x, p, positions, g_mix_pre, w_in, attn_sinks, hgrn_lb_logits, hgrn_gnorm,
           w_attn_branch, w_hgrn_branch, w_out, g_mix_post, g_ffn_pre, w_gate_up, w_down,
           g_ffn_post, g_ple_pre, w_ple_gate, w_ple_proj, g_ple_post):
    batch, seq, d = x.shape
    t = batch * seq
    depth = w_in.shape[0]
    row = lambda v: v.reshape(1, -1)

    xs = x.reshape(t, d)
    for layer in range(depth):
        w_rot = jnp.roll(w_in[layer], GATE_WIDTH, axis=1).astype(BF16)
        proj = _inproj(xs, row(g_mix_pre[layer]), w_rot)
        y_attn = _attention(proj, positions, attn_sinks[layer], batch, seq)
        y_hgrn = _hgrn(proj, hgrn_lb_logits, row(hgrn_gnorm[layer]), layer, batch, seq)
        xs = _merge(xs, y_attn, y_hgrn, proj, w_attn_branch[layer].astype(BF16),
                    w_hgrn_branch[layer].astype(BF16), w_out[layer].astype(BF16),
                    row(g_mix_post[layer]))
        xs = _ffn(xs, row(g_ffn_pre[layer]), w_gate_up[layer].astype(BF16),
                  w_down[layer].astype(BF16), row(g_ffn_post[layer]))
        xs = _ple(xs, p[layer].reshape(t, PLE_DIM), row(g_ple_pre[layer]),
                  w_ple_gate[layer].astype(BF16), w_ple_proj[layer].astype(BF16),
                  row(g_ple_post[layer]))
    return xs.reshape(batch, seq, d)
```

```python
import functools
import math

import numpy as np
import jax
import jax.numpy as jnp
from jax import lax
from jax.experimental import pallas as pl
from jax.experimental.pallas import tpu as pltpu

D_MODEL = 2048
PLE_DIM = 256
ATTN_HEADS = 16
ATTN_KV_HEADS = 4
ATTN_HEAD_DIM = 64
ATTN_Q_WIDTH = ATTN_HEADS * ATTN_HEAD_DIM
ATTN_KV_WIDTH = ATTN_KV_HEADS * ATTN_HEAD_DIM
ATTN_BLOCK = 128
ATTN_BLOCKS_PER_STEP = 4
ROPE_THETA = 10000.0
HGRN_HEADS = 8
HGRN_DK = 128
HGRN_DV = 128
HGRN_WIDTH = HGRN_HEADS * HGRN_DK
D_FF = 5632
RMS_EPS = 1e-6
IN_SIZES = (ATTN_Q_WIDTH, ATTN_KV_WIDTH, ATTN_KV_WIDTH,
            HGRN_WIDTH, HGRN_WIDTH, HGRN_WIDTH, HGRN_WIDTH, D_MODEL, D_MODEL)
IN_WIDTH = sum(IN_SIZES)
PROJ_OFFSETS = tuple(int(v) for v in np.cumsum((0,) + IN_SIZES[:-1]))

LANES = 128
SUBLANES = 8
HGRN_CHUNK = 128
HGRN_LEVELS = (1, 2, 4, 8, 16, 32, 64)
HGRN_HEADS_PER_STEP = 4
MXU_SUBTILE = 512
VMEM_LIMIT_BYTES = 58 * 1024 * 1024
LOG2E = math.log2(math.e)
MASKED_SCORE = -1e30

BF16 = jnp.bfloat16
F32 = jnp.float32


def _rms_scale(v):
    return lax.rsqrt(jnp.mean(v * v, axis=-1, keepdims=True) + RMS_EPS)


def _dot(a, b):
    return jnp.dot(a, b, preferred_element_type=F32)


def _block_index(offset, width):
    assert offset % width == 0, (offset, width)
    return offset // width


def _dot_nt(a, b):
    return lax.dot_general(a, b, (((1,), (1,)), ((), ())), preferred_element_type=F32)


def _inproj_kernel(x_ref, g_ref, w_ref, o_ref, h_ref):
    @pl.when(pl.program_id(1) == 0)
    def _():
        x = x_ref[...]
        h_ref[...] = (x * _rms_scale(x) * g_ref[...]).astype(BF16)

    for c0 in range(0, o_ref.shape[1], MXU_SUBTILE):
        cols = slice(c0, min(c0 + MXU_SUBTILE, o_ref.shape[1]))
        o_ref[:, cols] = _dot(h_ref[...], w_ref[:, cols]).astype(o_ref.dtype)


def _inproj(x2, g, w, tm=1024, tn=2432):
    t, d = x2.shape
    n = w.shape[1]
    assert t % tm == 0 and n % tn == 0
    return pl.pallas_call(
        _inproj_kernel,
        grid=(t // tm, n // tn),
        in_specs=[pl.BlockSpec((tm, d), lambda i, j: (i, 0)),
                  pl.BlockSpec((1, d), lambda i, j: (0, 0)),
                  pl.BlockSpec((d, tn), lambda i, j: (0, j))],
        out_specs=pl.BlockSpec((tm, tn), lambda i, j: (i, j)),
        out_shape=jax.ShapeDtypeStruct((t, n), BF16),
        scratch_shapes=[pltpu.VMEM((tm, d), BF16)],
        compiler_params=pltpu.CompilerParams(
            dimension_semantics=("arbitrary", "arbitrary"),
            vmem_limit_bytes=VMEM_LIMIT_BYTES),
        name="in_proj",
    )(x2, g, w)


def _attn_kernel(sink_ref, pos_ref, invf_ref, ctile_ref, stile_ref, q_ref, k_ref, v_ref, o_ref,
                 kz_ref, vz_ref):
    blk = ATTN_BLOCK
    nblk = q_ref.shape[0] // blk
    n = pl.program_id(1)
    carry_in = nblk + n % 2
    carry_out = nblk + (n + 1) % 2

    @pl.when(n == 0)
    def _():
        kz_ref[nblk] = jnp.zeros(kz_ref.shape[1:], kz_ref.dtype)
        vz_ref[nblk] = jnp.zeros(vz_ref.shape[1:], vz_ref.dtype)

    ang_t = invf_ref[...] * pos_ref[0]
    q_scale = LOG2E * ATTN_HEAD_DIM ** -0.5

    def spread(tab_t, tile_ref):
        p1 = tab_t.astype(BF16)
        r1 = tab_t - p1.astype(F32)
        p2 = r1.astype(BF16)
        p3 = (r1 - p2.astype(F32)).astype(BF16)
        stacked = jnp.concatenate([p1, p2, p3, jnp.zeros_like(p1)], axis=0)
        return _dot(stacked.astype(F32).T.astype(BF16), tile_ref[...])

    cos_t = jnp.cos(ang_t)
    sin_t = jnp.sin(ang_t)
    cos = spread(cos_t, ctile_ref)
    sin_signed = spread(sin_t, stile_ref)
    cos_q = spread(cos_t * q_scale, ctile_ref)
    sin_q = spread(sin_t * q_scale, stile_ref)

    lane = lax.broadcasted_iota(jnp.int32, (blk, LANES), 1)
    row = lax.broadcasted_iota(jnp.int32, (blk, LANES), 0)
    first_half = (lane & (ATTN_HEAD_DIM // 2)) == 0
    low_head = lane < ATTN_HEAD_DIM
    from_prev = lane > row

    def rope(t, j, c, s):
        rows = slice(j * blk, (j + 1) * blk)
        partner = jnp.where(first_half,
                            pltpu.roll(t, LANES - ATTN_HEAD_DIM // 2, 1),
                            pltpu.roll(t, ATTN_HEAD_DIM // 2, 1))
        return t * c[rows] + partner * s[rows]

    def place(src, dst, j, c):
        slots = (j, carry_out) if j == nblk - 1 else (j,)
        for e in range(2):
            own = jnp.where(low_head if e == 0 else jnp.logical_not(low_head), src, 0.0)
            other = pltpu.roll(own, ATTN_HEAD_DIM, 1)
            lo, hi = (own, other) if e == 0 else (other, own)
            for slot in slots:
                dst[slot, 2 * c + e, 0:blk, 0:LANES] = lo.astype(BF16)
                dst[slot, 2 * c + e, blk:2 * blk, 0:LANES] = hi.astype(BF16)

    def place_keys(j):
        rows = slice(j * blk, (j + 1) * blk)
        for c in range(ATTN_KV_WIDTH // LANES):
            place(rope(k_ref[rows, c * LANES:(c + 1) * LANES].astype(F32), j, cos, sin_signed),
                  kz_ref, j, c)

    def place_values(j):
        rows = slice(j * blk, (j + 1) * blk)
        ones_lo = low_head.astype(BF16)
        for slot in ((j, carry_out) if j == nblk - 1 else (j,)):
            for hk in range(ATTN_KV_HEADS):
                vz_ref[slot, hk, 0:blk, LANES:2 * LANES] = ones_lo
                vz_ref[slot, hk, blk:2 * blk, LANES:2 * LANES] = 1 - ones_lo
        for c in range(ATTN_KV_WIDTH // LANES):
            place(v_ref[rows, c * LANES:(c + 1) * LANES].astype(F32), vz_ref, j, c)

    first_penalty = jnp.where(n > 0, 0.0, MASKED_SCORE)
    items = [(j, hk) for j in range(nblk) for hk in range(ATTN_KV_HEADS)]

    def scores(j, hk):
        rows = slice(j * blk, (j + 1) * blk)
        q = jnp.concatenate(
            [rope(q_ref[rows, g * LANES:(g + 1) * LANES].astype(F32), j, cos_q, sin_q).astype(BF16)
             for g in (2 * hk, 2 * hk + 1)], axis=0)
        prv = carry_in if j == 0 else j - 1
        return _dot_nt(q, kz_ref[prv, hk]), _dot_nt(q, kz_ref[j, hk])

    def fold_and_max(j, hk, s_prev, s_cur):
        folded = []
        for rh in range(2):
            rows = slice(rh * blk, (rh + 1) * blk)
            for ch in range(2):
                sink = sink_ref[4 * hk + 2 * rh + ch] * LOG2E
                cols = slice(ch * blk, (ch + 1) * blk)
                sp = s_prev[rows, cols] + first_penalty if j == 0 else s_prev[rows, cols]
                s = jnp.where(from_prev, sp, s_cur[rows, cols])
                folded.append((s, jnp.maximum(jnp.max(s, axis=-1, keepdims=True), sink), sink))
        return folded

    def weigh(j, hk, folded):
        ps, sink_terms = [], []
        for s, m, sink in folded:
            ps.append(jnp.exp2(s - m))
            sink_terms.append(jnp.exp2(sink - m))
        stack = lambda tiles: jnp.concatenate(
            [jnp.concatenate(tiles[0:2], axis=1), jnp.concatenate(tiles[2:4], axis=1)], axis=0)
        p_prev = stack([jnp.where(from_prev, p, 0.0).astype(BF16) for p in ps])
        p_cur = stack([jnp.where(from_prev, 0.0, p).astype(BF16) for p in ps])
        prv = carry_in if j == 0 else j - 1
        acc = _dot(p_prev, vz_ref[prv, hk]) + _dot(p_cur, vz_ref[j, hk])
        return acc, sink_terms

    def finish(j, hk, acc, sink_terms):
        for rh in range(2):
            rows = slice(rh * blk, (rh + 1) * blk)
            g = 2 * hk + rh
            denom = acc[rows, LANES:] + jnp.where(low_head, sink_terms[2 * rh],
                                                  sink_terms[2 * rh + 1])
            o_ref[j * blk:(j + 1) * blk, g * LANES:(g + 1) * LANES] = (
                acc[rows, :LANES] / denom).astype(o_ref.dtype)

    place_keys(0)
    s_vals, f_vals, w_vals = {}, {}, {}
    for k in range(len(items) + 3):
        if k < len(items):
            j, hk = items[k]
            s_vals[k] = scores(j, hk)
            if hk == 0:
                place_values(j)
            if hk == 1 and j + 1 < nblk:
                place_keys(j + 1)
        if 0 <= k - 1 < len(items):
            f_vals[k - 1] = fold_and_max(*items[k - 1], *s_vals.pop(k - 1))
        if 0 <= k - 2 < len(items):
            w_vals[k - 2] = weigh(*items[k - 2], f_vals.pop(k - 2))
        if 0 <= k - 3 < len(items):
            finish(*items[k - 3], *w_vals.pop(k - 3))


def _rope_constants():
    half = ATTN_HEAD_DIM // 2
    inv_freq = ROPE_THETA ** (-np.arange(half, dtype=np.float32) / half)
    k = np.arange(LANES)[:, None]
    lane = np.arange(LANES)[None, :]
    hit = ((k < 3 * half) & (k % half == lane % half)).astype(np.float32)
    sign = np.where(lane % ATTN_HEAD_DIM < half, -1.0, 1.0).astype(np.float32)
    return inv_freq.reshape(half, 1), hit, hit * sign


def _attention(proj, positions, sinks, batch, seq):
    blk = ATTN_BLOCK
    rows = ATTN_BLOCKS_PER_STEP * blk
    assert seq % rows == 0
    nb = seq // rows
    t = batch * seq
    inv_freq, ctile, stile = _rope_constants()
    pos_rows = positions.astype(F32).reshape(t // rows, 1, rows)
    const = lambda a: pl.BlockSpec(a.shape, lambda b, n: (0, 0))
    q_blk, k_blk, v_blk = (_block_index(PROJ_OFFSETS[0], ATTN_Q_WIDTH),
                           _block_index(PROJ_OFFSETS[1], ATTN_KV_WIDTH),
                           _block_index(PROJ_OFFSETS[2], ATTN_KV_WIDTH))
    slots = ATTN_BLOCKS_PER_STEP + 2
    k_scratch = pltpu.VMEM((slots, ATTN_KV_HEADS, 2 * blk, LANES), BF16)
    v_scratch = pltpu.VMEM((slots, ATTN_KV_HEADS, 2 * blk, 2 * LANES), BF16)
    return pl.pallas_call(
        _attn_kernel,
        grid=(batch, nb),
        in_specs=[pl.BlockSpec(memory_space=pltpu.SMEM),
                  pl.BlockSpec((1, 1, rows), lambda b, n: (b * nb + n, 0, 0)),
                  const(inv_freq), const(ctile), const(stile),
                  pl.BlockSpec((rows, ATTN_Q_WIDTH), lambda b, n: (b * nb + n, q_blk)),
                  pl.BlockSpec((rows, ATTN_KV_WIDTH), lambda b, n: (b * nb + n, k_blk)),
                  pl.BlockSpec((rows, ATTN_KV_WIDTH), lambda b, n: (b * nb + n, v_blk))],
        out_specs=pl.BlockSpec((rows, ATTN_Q_WIDTH), lambda b, n: (b * nb + n, 0)),
        out_shape=jax.ShapeDtypeStruct((t, ATTN_Q_WIDTH), BF16),
        scratch_shapes=[k_scratch, v_scratch],
        compiler_params=pltpu.CompilerParams(
            dimension_semantics=("arbitrary", "arbitrary")),
        name="swa_attention",
    )(sinks, pos_rows, jnp.asarray(inv_freq), jnp.asarray(ctile, dtype=BF16),
      jnp.asarray(stile, dtype=BF16), proj, proj, proj)


def _hgrn_constants():
    c = HGRN_CHUNK
    u = np.arange(c)[None, :]
    t = np.arange(c)[:, None]
    ranges, masks = [], []
    for m in HGRN_LEVELS:
        upper = (t & m) != 0
        start = (t // m) * m
        q_rng = (u >= start) & (u <= t)
        k_rng = (u > t) & (u <= start + m - 1)
        if m < SUBLANES:
            ranges.append(np.where(upper, q_rng, k_rng))
        masks.append(upper & ((u & m) == 0) & ((t // (2 * m)) == (u // (2 * m))))
    ranges.append(u <= t)
    masks.append(u == t)
    rng = np.concatenate(ranges, axis=0).astype(np.float32)
    return np.concatenate([rng, rng], axis=1), np.concatenate(masks, axis=0).astype(np.float32)


def _hgrn_kernel(lbl_ref, gn_ref, w_ref, m_ref, q_ref, f_ref, i_ref, g_ref, o_ref, st_ref,
                 *, layer):
    c = HGRN_CHUNK
    nl = len(HGRN_LEVELS)
    n_fine = sum(m < SUBLANES for m in HGRN_LEVELS)
    heads = q_ref.shape[1] // HGRN_DK
    n_chunks = q_ref.shape[0] // c

    @pl.when(pl.program_id(2) == 0)
    def _():
        st_ref[...] = jnp.zeros_like(st_ref)

    lg = lbl_ref[...]
    eg = jnp.exp(lg - jnp.max(lg, axis=0, keepdims=True))
    lb_all = (jnp.sum(eg[0:layer + 1, :], axis=0, keepdims=True)
              / jnp.sum(eg, axis=0, keepdims=True))
    gn = gn_ref[...]

    head_cols = [slice(h * HGRN_DK, (h + 1) * HGRN_DK) for h in range(heads)]

    def gates(ci):
        rows = slice(ci * c, (ci + 1) * c)
        qts, kks, pieces = [], [], []
        for cols in head_cols:
            lb = lb_all[:, cols]
            z = f_ref[rows, cols].astype(F32)
            qr = q_ref[rows, cols].astype(F32)
            sig = jax.nn.sigmoid(z)
            logf = jnp.log2(lb + (1.0 - lb) * sig)
            kks.append((1.0 - lb) * (1.0 - sig))
            qts.append(qr * jax.nn.sigmoid(qr) * (HGRN_DK ** -0.5))
            hi = logf.astype(BF16)
            pieces.append((hi, (logf - hi.astype(F32)).astype(BF16)))
        split = jnp.concatenate(
            [jnp.concatenate([p[0] for p in pieces], axis=1),
             jnp.concatenate([p[1] for p in pieces], axis=1)], axis=0)
        return qts, kks, _dot(w_ref[...], split)

    def coarse_exponent(b, m):
        pieces = []
        for lo in range(0, c, 2 * m):
            ref = b[lo + m - 1:lo + m, :]
            pieces += [ref - b[lo:lo + m, :], b[lo + m:lo + 2 * m, :] - ref]
        return jnp.concatenate(pieces, axis=0)

    def intra(ci, qts, kks, xs):
        parts = []
        for h, cols in enumerate(head_cols):
            qt, kk = qts[h], kks[h]
            b = xs[n_fine * c:(n_fine + 1) * c, cols]
            a = m_ref[nl * c:(nl + 1) * c, :] * _dot_nt(qt.astype(BF16), kk.astype(BF16))
            for li, m in enumerate(HGRN_LEVELS):
                x = xs[li * c:(li + 1) * c, cols] if li < n_fine else coarse_exponent(b, m)
                e = jnp.exp2(x)
                pair = _dot_nt((qt * e).astype(BF16), (kk * e).astype(BF16))
                a = a + m_ref[li * c:(li + 1) * c, :] * pair
            eb = jnp.exp2(b)
            el = jnp.exp2(b[c - 1:c, :] - b)
            v = i_ref[ci * c:(ci + 1) * c, cols]
            parts.append(((qt * eb).astype(BF16), _dot(a.astype(BF16), v), eb[c - 1:c, :],
                          _dot(v.astype(F32).T.astype(BF16), (kk * el).astype(BF16))))
        return parts

    def carry_state(ci, parts, states):
        rows = slice(ci * c, (ci + 1) * c)
        new_states = []
        for h, cols in enumerate(head_cols):
            q_decayed, o_intra, chunk_decay, kv = parts[h]
            st = states[h]
            o = _dot_nt(q_decayed, st.astype(BF16)) + o_intra
            new_states.append(st * chunk_decay + kv)
            gate = g_ref[rows, cols].astype(F32)
            y = o * _rms_scale(o) * gn
            y = y * (gate * jax.nn.sigmoid(gate))
            o_ref[rows, cols] = y.astype(o_ref.dtype)
        return new_states

    states = [st_ref[h] for h in range(heads)]
    g_vals, i_vals = {0: gates(0)}, {}
    for k in range(n_chunks + 1):
        if k + 1 < n_chunks:
            g_vals[k + 1] = gates(k + 1)
        if k < n_chunks:
            i_vals[k] = intra(k, *g_vals.pop(k))
        if k >= 1:
            states = carry_state(k - 1, i_vals.pop(k - 1), states)
    for h in range(heads):
        st_ref[h] = states[h]


def _hgrn(proj, lb_logits, gnorm, layer, batch, seq, rows=1024):
    t = batch * seq
    steps = seq // rows
    hp = HGRN_HEADS_PER_STEP
    width = hp * HGRN_DK
    q_blk, f_blk, i_blk, g_blk = (_block_index(PROJ_OFFSETS[s], width) for s in (3, 4, 5, 6))
    rng, masks = _hgrn_constants()
    rng = jnp.asarray(rng, dtype=BF16)
    masks = jnp.asarray(masks, dtype=F32)
    const = lambda a: pl.BlockSpec(a.shape, lambda b, h, s: (0, 0))
    row_spec = lambda base: pl.BlockSpec(
        (rows, width), lambda b, h, s, base=base: (b * steps + s, base + h))
    return pl.pallas_call(
        functools.partial(_hgrn_kernel, layer=layer),
        grid=(batch, HGRN_HEADS // hp, steps),
        in_specs=[pl.BlockSpec((lb_logits.shape[0], width), lambda b, h, s: (0, h)),
                  pl.BlockSpec((1, HGRN_DV), lambda b, h, s: (0, 0)),
                  const(rng), const(masks),
                  row_spec(q_blk), row_spec(f_blk), row_spec(i_blk), row_spec(g_blk)],
        out_specs=pl.BlockSpec((rows, width), lambda b, h, s: (b * steps + s, h)),
        out_shape=jax.ShapeDtypeStruct((t, HGRN_WIDTH), BF16),
        scratch_shapes=[pltpu.VMEM((hp, HGRN_DV, HGRN_DK), F32)],
        compiler_params=pltpu.CompilerParams(
            dimension_semantics=("arbitrary", "arbitrary", "arbitrary")),
        name="hgrn2",
    )(lb_logits, gnorm, rng, masks, proj, proj, proj, proj)


def _merge_kernel(x_ref, ya_ref, yh_ref, wa_ref, wh_ref, wo_ref, gp_ref, *rest):
    n_tiles = (len(rest) - 2) // 2
    ga_refs, gb_refs = rest[:n_tiles], rest[n_tiles:2 * n_tiles]
    o_ref, merged_ref = rest[2 * n_tiles:]
    ya, yh = ya_ref[...], yh_ref[...]
    tile_cols = [slice(c * MXU_SUBTILE, (c + 1) * MXU_SUBTILE) for c in range(n_tiles)]
    for cols, ga_ref, gb_ref in zip(tile_cols, ga_refs, gb_refs):
        merged_ref[:, cols] = (
            jax.nn.sigmoid(ga_ref[...].astype(F32)) * _dot(ya, wa_ref[:, cols])
            + jax.nn.sigmoid(gb_ref[...].astype(F32)) * _dot(yh, wh_ref[:, cols])).astype(BF16)
    merged = merged_ref[...]
    for cols in tile_cols:
        o_ref[:, cols] = _dot(merged, wo_ref[:, cols])
    m2 = o_ref[...]
    o_ref[...] = x_ref[...] + m2 * _rms_scale(m2) * gp_ref[...]


def _merge(x2, y_attn, y_hgrn, proj, wa, wh, wo, g_post, tm=512):
    t, d = x2.shape
    n_tiles = d // MXU_SUBTILE
    ga_blk, gb_blk = (_block_index(PROJ_OFFSETS[s], MXU_SUBTILE) for s in (7, 8))
    const = lambda shape: pl.BlockSpec(shape, lambda i: (0, 0), pipeline_mode=pl.Buffered(1))
    gate_specs = [pl.BlockSpec((tm, MXU_SUBTILE), lambda i, blk=base + c: (i, blk))
                  for base in (ga_blk, gb_blk) for c in range(n_tiles)]
    return pl.pallas_call(
        _merge_kernel,
        grid=(t // tm,),
        in_specs=[pl.BlockSpec((tm, d), lambda i: (i, 0)),
                  pl.BlockSpec((tm, ATTN_Q_WIDTH), lambda i: (i, 0)),
                  pl.BlockSpec((tm, HGRN_WIDTH), lambda i: (i, 0)),
                  const(wa.shape), const(wh.shape), const(wo.shape), const((1, d))] + gate_specs,
        out_specs=pl.BlockSpec((tm, d), lambda i: (i, 0)),
        out_shape=jax.ShapeDtypeStruct((t, d), F32),
        scratch_shapes=[pltpu.VMEM((tm, d), BF16)],
        compiler_params=pltpu.CompilerParams(
            dimension_semantics=("arbitrary",), vmem_limit_bytes=VMEM_LIMIT_BYTES),
        name="merge_out",
    )(x2, y_attn, y_hgrn, wa, wh, wo, g_post, *([proj] * (2 * n_tiles)))


def _ffn_kernel(x_ref, gpre_ref, wg_ref, wu_ref, wd_ref, gpost_ref, o_ref, h_ref):
    j = pl.program_id(1)

    @pl.when(j == 0)
    def _():
        x = x_ref[...]
        h_ref[...] = (x * _rms_scale(x) * gpre_ref[...]).astype(BF16)
        o_ref[...] = jnp.zeros_like(o_ref)

    h = h_ref[...]
    gate = _dot(h, wg_ref[...])
    up = _dot(h, wu_ref[...])
    act = (gate * jax.nn.sigmoid(gate) * up).astype(BF16)
    for c0 in range(0, o_ref.shape[1], MXU_SUBTILE):
        cols = slice(c0, c0 + MXU_SUBTILE)
        o_ref[:, cols] += _dot(act, wd_ref[:, cols])

    @pl.when(j == pl.num_programs(1) - 1)
    def _():
        y = o_ref[...]
        o_ref[...] = x_ref[...] + y * _rms_scale(y) * gpost_ref[...]


def _ffn(x1, g_pre, w_gate_up, w_down, g_post, tm=512, tf=512):
    t, d = x1.shape
    nf = D_FF // tf
    return pl.pallas_call(
        _ffn_kernel,
        grid=(t // tm, nf),
        in_specs=[pl.BlockSpec((tm, d), lambda i, j: (i, 0)),
                  pl.BlockSpec((1, d), lambda i, j: (0, 0)),
                  pl.BlockSpec((d, tf), lambda i, j: (0, j)),
                  pl.BlockSpec((d, tf), lambda i, j: (0, j + nf)),
                  pl.BlockSpec((tf, d), lambda i, j: (j, 0)),
                  pl.BlockSpec((1, d), lambda i, j: (0, 0))],
        out_specs=pl.BlockSpec((tm, d), lambda i, j: (i, 0)),
        out_shape=jax.ShapeDtypeStruct((t, d), F32),
        scratch_shapes=[pltpu.VMEM((tm, d), BF16)],
        compiler_params=pltpu.CompilerParams(
            dimension_semantics=("arbitrary", "arbitrary"),
            vmem_limit_bytes=VMEM_LIMIT_BYTES),
        name="swiglu_ffn",
    )(x1, g_pre, w_gate_up, w_gate_up, w_down, g_post)


def _ple_kernel(x_ref, p_ref, gpre_ref, wg_ref, wp_ref, gpost_ref, o_ref):
    x = x_ref[...]
    h = (x * _rms_scale(x) * gpre_ref[...]).astype(BF16)
    p16 = p_ref[...].astype(BF16)
    for c0 in range(0, o_ref.shape[1], MXU_SUBTILE):
        cols = slice(c0, c0 + MXU_SUBTILE)
        gate = jax.nn.sigmoid(_dot(h, wg_ref[:, cols]))
        o_ref[:, cols] = _dot(p16, wp_ref[:, cols]) * gate
    e = o_ref[...]
    o_ref[...] = x_ref[...] + e * _rms_scale(e) * gpost_ref[...]


def _ple(x2, p2, g_pre, w_gate, w_proj, g_post, tm=512):
    t, d = x2.shape
    const = lambda shape: pl.BlockSpec(shape, lambda i: (0, 0), pipeline_mode=pl.Buffered(1))
    return pl.pallas_call(
        _ple_kernel,
        grid=(t // tm,),
        in_specs=[pl.BlockSpec((tm, d), lambda i: (i, 0)),
                  pl.BlockSpec((tm, PLE_DIM), lambda i: (i, 0)),
                  const((1, d)), const(w_gate.shape), const(w_proj.shape), const((1, d))],
        out_specs=pl.BlockSpec((tm, d), lambda i: (i, 0)),
        out_shape=jax.ShapeDtypeStruct((t, d), F32),
        compiler_params=pltpu.CompilerParams(
            dimension_semantics=("arbitrary",), vmem_limit_bytes=VMEM_LIMIT_BYTES),
        name="ple",
    )(x2, p2, g_pre, w_gate, w_proj, g_post)


def kernel(x, p, positions, g_mix_pre, w_in, attn_sinks, hgrn_lb_logits, hgrn_gnorm,
           w_attn_branch, w_hgrn_branch, w_out, g_mix_post, g_ffn_pre, w_gate_up, w_down,
           g_ffn_post, g_ple_pre, w_ple_gate, w_ple_proj, g_ple_post):
    batch, seq, d = x.shape
    t = batch * seq
    depth = w_in.shape[0]
    row = lambda v: v.reshape(1, -1)

    xs = x.reshape(t, d)
    for layer in range(depth):
        proj = _inproj(xs, row(g_mix_pre[layer]), w_in[layer].astype(BF16))
        y_attn = _attention(proj, positions, attn_sinks[layer], batch, seq)
        y_hgrn = _hgrn(proj, hgrn_lb_logits, row(hgrn_gnorm[layer]), layer, batch, seq)
        xs = _merge(xs, y_attn, y_hgrn, proj, w_attn_branch[layer].astype(BF16),
                    w_hgrn_branch[layer].astype(BF16), w_out[layer].astype(BF16),
                    row(g_mix_post[layer]))
        xs = _ffn(xs, row(g_ffn_pre[layer]), w_gate_up[layer].astype(BF16),
                  w_down[layer].astype(BF16), row(g_ffn_post[layer]))
        xs = _ple(xs, p[layer].reshape(t, PLE_DIM), row(g_ple_pre[layer]),
                  w_ple_gate[layer].astype(BF16), w_ple_proj[layer].astype(BF16),
                  row(g_ple_post[layer]))
    return xs.reshape(batch, seq, d)
```

```python
import functools
import math

import numpy as np
import jax
import jax.numpy as jnp
from jax import lax
from jax.experimental import pallas as pl
from jax.experimental.pallas import tpu as pltpu

D_MODEL = 2048
PLE_DIM = 256
ATTN_HEADS = 16
ATTN_KV_HEADS = 4
ATTN_HEAD_DIM = 64
ATTN_Q_WIDTH = ATTN_HEADS * ATTN_HEAD_DIM
ATTN_KV_WIDTH = ATTN_KV_HEADS * ATTN_HEAD_DIM
ATTN_BLOCK = 128
ATTN_BLOCKS_PER_STEP = 4
ROPE_THETA = 10000.0
HGRN_HEADS = 8
HGRN_DK = 128
HGRN_DV = 128
HGRN_WIDTH = HGRN_HEADS * HGRN_DK
D_FF = 5632
RMS_EPS = 1e-6
IN_SIZES = (ATTN_Q_WIDTH, ATTN_KV_WIDTH, ATTN_KV_WIDTH,
            HGRN_WIDTH, HGRN_WIDTH, HGRN_WIDTH, HGRN_WIDTH, D_MODEL, D_MODEL)
IN_WIDTH = sum(IN_SIZES)
PROJ_OFFSETS = tuple(int(v) for v in np.cumsum((0,) + IN_SIZES[:-1]))

LANES = 128
SUBLANES = 8
HGRN_CHUNK = 128
HGRN_LEVELS = (1, 2, 4, 8, 16, 32, 64)
HGRN_HEADS_PER_STEP = 4
MXU_SUBTILE = 512
VMEM_LIMIT_BYTES = 58 * 1024 * 1024
LOG2E = math.log2(math.e)
MASKED_SCORE = -1e30

BF16 = jnp.bfloat16
F32 = jnp.float32


def _rms_scale(v):
    return lax.rsqrt(jnp.mean(v * v, axis=-1, keepdims=True) + RMS_EPS)


def _sigmoid(v):
    return 0.5 * jnp.tanh(0.5 * v) + 0.5


def _dot(a, b):
    return jnp.dot(a, b, preferred_element_type=F32)


def _block_index(offset, width):
    assert offset % width == 0, (offset, width)
    return offset // width


def _dot_nt(a, b):
    return lax.dot_general(a, b, (((1,), (1,)), ((), ())), preferred_element_type=F32)


def _inproj_kernel(x_ref, g_ref, w_ref, o_ref, h_ref):
    @pl.when(pl.program_id(1) == 0)
    def _():
        x = x_ref[...]
        h_ref[...] = (x * _rms_scale(x) * g_ref[...]).astype(BF16)

    for c0 in range(0, o_ref.shape[1], MXU_SUBTILE):
        cols = slice(c0, min(c0 + MXU_SUBTILE, o_ref.shape[1]))
        o_ref[:, cols] = _dot(h_ref[...], w_ref[:, cols]).astype(o_ref.dtype)


def _inproj(x2, g, w, tm=1024, tn=2432):
    t, d = x2.shape
    n = w.shape[1]
    assert t % tm == 0 and n % tn == 0
    return pl.pallas_call(
        _inproj_kernel,
        grid=(t // tm, n // tn),
        in_specs=[pl.BlockSpec((tm, d), lambda i, j: (i, 0)),
                  pl.BlockSpec((1, d), lambda i, j: (0, 0)),
                  pl.BlockSpec((d, tn), lambda i, j: (0, j))],
        out_specs=pl.BlockSpec((tm, tn), lambda i, j: (i, j)),
        out_shape=jax.ShapeDtypeStruct((t, n), BF16),
        scratch_shapes=[pltpu.VMEM((tm, d), BF16)],
        compiler_params=pltpu.CompilerParams(
            dimension_semantics=("arbitrary", "arbitrary"),
            vmem_limit_bytes=VMEM_LIMIT_BYTES),
        name="in_proj",
    )(x2, g, w)


def _attn_kernel(sink_ref, pos_ref, invf_ref, ctile_ref, stile_ref, q_ref, k_ref, v_ref, *rest):
    n_w = (len(rest) - 3) // 2
    o_ref, (kz_ref, vz_ref) = rest[n_w], rest[-2:]
    for w_ref, w16_ref in zip(rest[:n_w], rest[n_w + 1:2 * n_w + 1]):
        w16_ref[...] = w_ref[...].astype(w16_ref.dtype)

    blk = ATTN_BLOCK
    nblk = q_ref.shape[0] // blk
    n = pl.program_id(1)
    carry_in = nblk + n % 2
    carry_out = nblk + (n + 1) % 2

    @pl.when(n == 0)
    def _():
        kz_ref[nblk] = jnp.zeros(kz_ref.shape[1:], kz_ref.dtype)
        vz_ref[nblk] = jnp.zeros(vz_ref.shape[1:], vz_ref.dtype)

    ang_t = invf_ref[...] * pos_ref[0]
    q_scale = LOG2E * ATTN_HEAD_DIM ** -0.5

    def spread(tab_t, tile_ref):
        p1 = tab_t.astype(BF16)
        r1 = tab_t - p1.astype(F32)
        p2 = r1.astype(BF16)
        p3 = (r1 - p2.astype(F32)).astype(BF16)
        stacked = jnp.concatenate([p1, p2, p3, jnp.zeros_like(p1)], axis=0)
        return _dot(stacked.astype(F32).T.astype(BF16), tile_ref[...])

    cos_t = jnp.cos(ang_t)
    sin_t = jnp.sin(ang_t)
    cos = spread(cos_t, ctile_ref)
    sin_signed = spread(sin_t, stile_ref)
    cos_q = spread(cos_t * q_scale, ctile_ref)
    sin_q = spread(sin_t * q_scale, stile_ref)

    lane = lax.broadcasted_iota(jnp.int32, (blk, LANES), 1)
    row = lax.broadcasted_iota(jnp.int32, (blk, LANES), 0)
    first_half = (lane & (ATTN_HEAD_DIM // 2)) == 0
    low_head = lane < ATTN_HEAD_DIM
    from_prev = lane > row

    def rope(t, j, c, s):
        rows = slice(j * blk, (j + 1) * blk)
        partner = jnp.where(first_half,
                            pltpu.roll(t, LANES - ATTN_HEAD_DIM // 2, 1),
                            pltpu.roll(t, ATTN_HEAD_DIM // 2, 1))
        return t * c[rows] + partner * s[rows]

    def place(src, dst, j, c):
        slots = (j, carry_out) if j == nblk - 1 else (j,)
        for e in range(2):
            own = jnp.where(low_head if e == 0 else jnp.logical_not(low_head), src, 0.0)
            other = pltpu.roll(own, ATTN_HEAD_DIM, 1)
            lo, hi = (own, other) if e == 0 else (other, own)
            for slot in slots:
                dst[slot, 2 * c + e, 0:blk, 0:LANES] = lo.astype(BF16)
                dst[slot, 2 * c + e, blk:2 * blk, 0:LANES] = hi.astype(BF16)

    def place_keys(j):
        rows = slice(j * blk, (j + 1) * blk)
        for c in range(ATTN_KV_WIDTH // LANES):
            place(rope(k_ref[rows, c * LANES:(c + 1) * LANES].astype(F32), j, cos, sin_signed),
                  kz_ref, j, c)

    def place_values(j):
        rows = slice(j * blk, (j + 1) * blk)
        ones_lo = low_head.astype(BF16)
        for slot in ((j, carry_out) if j == nblk - 1 else (j,)):
            for hk in range(ATTN_KV_HEADS):
                vz_ref[slot, hk, 0:blk, LANES:2 * LANES] = ones_lo
                vz_ref[slot, hk, blk:2 * blk, LANES:2 * LANES] = 1 - ones_lo
        for c in range(ATTN_KV_WIDTH // LANES):
            place(v_ref[rows, c * LANES:(c + 1) * LANES].astype(F32), vz_ref, j, c)

    first_penalty = jnp.where(n > 0, 0.0, MASKED_SCORE)
    items = [(j, hk) for j in range(nblk) for hk in range(ATTN_KV_HEADS)]

    def scores(j, hk):
        rows = slice(j * blk, (j + 1) * blk)
        q = jnp.concatenate(
            [rope(q_ref[rows, g * LANES:(g + 1) * LANES].astype(F32), j, cos_q, sin_q).astype(BF16)
             for g in (2 * hk, 2 * hk + 1)], axis=0)
        prv = carry_in if j == 0 else j - 1
        return _dot_nt(q, kz_ref[prv, hk]), _dot_nt(q, kz_ref[j, hk])

    def fold_and_max(j, hk, s_prev, s_cur):
        folded = []
        for rh in range(2):
            rows = slice(rh * blk, (rh + 1) * blk)
            for ch in range(2):
                sink = sink_ref[4 * hk + 2 * rh + ch] * LOG2E
                cols = slice(ch * blk, (ch + 1) * blk)
                sp = s_prev[rows, cols] + first_penalty if j == 0 else s_prev[rows, cols]
                s = jnp.where(from_prev, sp, s_cur[rows, cols])
                folded.append((s, jnp.maximum(jnp.max(s, axis=-1, keepdims=True), sink), sink))
        return folded

    def weigh(j, hk, folded):
        ps, sink_terms = [], []
        for s, m, sink in folded:
            ps.append(jnp.exp2(s - m))
            sink_terms.append(jnp.exp2(sink - m))
        stack = lambda tiles: jnp.concatenate(
            [jnp.concatenate(tiles[0:2], axis=1), jnp.concatenate(tiles[2:4], axis=1)], axis=0)
        p_prev = stack([jnp.where(from_prev, p, 0.0).astype(BF16) for p in ps])
        p_cur = stack([jnp.where(from_prev, 0.0, p).astype(BF16) for p in ps])
        prv = carry_in if j == 0 else j - 1
        acc = _dot(p_prev, vz_ref[prv, hk]) + _dot(p_cur, vz_ref[j, hk])
        return acc, sink_terms

    def finish(j, hk, acc, sink_terms):
        for rh in range(2):
            rows = slice(rh * blk, (rh + 1) * blk)
            g = 2 * hk + rh
            denom = acc[rows, LANES:] + jnp.where(low_head, sink_terms[2 * rh],
                                                  sink_terms[2 * rh + 1])
            o_ref[j * blk:(j + 1) * blk, g * LANES:(g + 1) * LANES] = (
                acc[rows, :LANES] / denom).astype(o_ref.dtype)

    place_keys(0)
    s_vals, f_vals, w_vals = {}, {}, {}
    for k in range(len(items) + 3):
        if k < len(items):
            j, hk = items[k]
            s_vals[k] = scores(j, hk)
            if hk == 0:
                place_values(j)
            if hk == 1 and j + 1 < nblk:
                place_keys(j + 1)
        if 0 <= k - 1 < len(items):
            f_vals[k - 1] = fold_and_max(*items[k - 1], *s_vals.pop(k - 1))
        if 0 <= k - 2 < len(items):
            w_vals[k - 2] = weigh(*items[k - 2], f_vals.pop(k - 2))
        if 0 <= k - 3 < len(items):
            finish(*items[k - 3], *w_vals.pop(k - 3))


def _rope_constants():
    half = ATTN_HEAD_DIM // 2
    inv_freq = ROPE_THETA ** (-np.arange(half, dtype=np.float32) / half)
    k = np.arange(LANES)[:, None]
    lane = np.arange(LANES)[None, :]
    hit = ((k < 3 * half) & (k % half == lane % half)).astype(np.float32)
    sign = np.where(lane % ATTN_HEAD_DIM < half, -1.0, 1.0).astype(np.float32)
    return inv_freq.reshape(half, 1), hit, hit * sign


def _attention(proj, positions, sinks, weights, batch, seq):
    blk = ATTN_BLOCK
    rows = ATTN_BLOCKS_PER_STEP * blk
    assert seq % rows == 0
    nb = seq // rows
    t = batch * seq
    steps = batch * nb
    for w in weights:
        assert w.shape[0] % (steps * 2 * SUBLANES) == 0, w.shape
    w_specs = [pl.BlockSpec((w.shape[0] // steps, w.shape[1]), lambda b, n: (b * nb + n, 0))
               for w in weights]
    inv_freq, ctile, stile = _rope_constants()
    pos_rows = positions.astype(F32).reshape(t // rows, 1, rows)
    const = lambda a: pl.BlockSpec(a.shape, lambda b, n: (0, 0))
    q_blk, k_blk, v_blk = (_block_index(PROJ_OFFSETS[0], ATTN_Q_WIDTH),
                           _block_index(PROJ_OFFSETS[1], ATTN_KV_WIDTH),
                           _block_index(PROJ_OFFSETS[2], ATTN_KV_WIDTH))
    slots = ATTN_BLOCKS_PER_STEP + 2
    k_scratch = pltpu.VMEM((slots, ATTN_KV_HEADS, 2 * blk, LANES), BF16)
    v_scratch = pltpu.VMEM((slots, ATTN_KV_HEADS, 2 * blk, 2 * LANES), BF16)
    outs = pl.pallas_call(
        _attn_kernel,
        grid=(batch, nb),
        in_specs=[pl.BlockSpec(memory_space=pltpu.SMEM),
                  pl.BlockSpec((1, 1, rows), lambda b, n: (b * nb + n, 0, 0)),
                  const(inv_freq), const(ctile), const(stile),
                  pl.BlockSpec((rows, ATTN_Q_WIDTH), lambda b, n: (b * nb + n, q_blk)),
                  pl.BlockSpec((rows, ATTN_KV_WIDTH), lambda b, n: (b * nb + n, k_blk)),
                  pl.BlockSpec((rows, ATTN_KV_WIDTH), lambda b, n: (b * nb + n, v_blk))] + w_specs,
        out_specs=[pl.BlockSpec((rows, ATTN_Q_WIDTH), lambda b, n: (b * nb + n, 0))] + w_specs,
        out_shape=[jax.ShapeDtypeStruct((t, ATTN_Q_WIDTH), BF16)]
        + [jax.ShapeDtypeStruct(w.shape, BF16) for w in weights],
        scratch_shapes=[k_scratch, v_scratch],
        compiler_params=pltpu.CompilerParams(
            dimension_semantics=("arbitrary", "arbitrary"),
            vmem_limit_bytes=VMEM_LIMIT_BYTES),
        name="swa_attention",
    )(sinks, pos_rows, jnp.asarray(inv_freq), jnp.asarray(ctile, dtype=BF16),
      jnp.asarray(stile, dtype=BF16), proj, proj, proj, *weights)
    return outs[0], outs[1:]


def _hgrn_constants():
    c = HGRN_CHUNK
    u = np.arange(c)[None, :]
    t = np.arange(c)[:, None]
    ranges, masks = [], []
    for m in HGRN_LEVELS:
        upper = (t & m) != 0
        start = (t // m) * m
        q_rng = (u >= start) & (u <= t)
        k_rng = (u > t) & (u <= start + m - 1)
        if m < SUBLANES:
            ranges.append(np.where(upper, q_rng, k_rng))
        masks.append(upper & ((u & m) == 0) & ((t // (2 * m)) == (u // (2 * m))))
    ranges.append(u <= t)
    masks.append(u == t)
    rng = np.concatenate(ranges, axis=0).astype(np.float32)
    return np.concatenate([rng, rng], axis=1), np.concatenate(masks, axis=0).astype(np.float32)


def _hgrn_kernel(lbl_ref, gn_ref, w_ref, m_ref, q_ref, f_ref, i_ref, g_ref, o_ref, st_ref,
                 *, layer):
    c = HGRN_CHUNK
    nl = len(HGRN_LEVELS)
    n_fine = sum(m < SUBLANES for m in HGRN_LEVELS)
    heads = q_ref.shape[1] // HGRN_DK
    n_chunks = q_ref.shape[0] // c

    @pl.when(pl.program_id(2) == 0)
    def _():
        st_ref[...] = jnp.zeros_like(st_ref)

    lg = lbl_ref[...]
    eg = jnp.exp(lg - jnp.max(lg, axis=0, keepdims=True))
    lb_all = (jnp.sum(eg[0:layer + 1, :], axis=0, keepdims=True)
              / jnp.sum(eg, axis=0, keepdims=True))
    gn = gn_ref[...]

    head_cols = [slice(h * HGRN_DK, (h + 1) * HGRN_DK) for h in range(heads)]

    def gates(ci):
        rows = slice(ci * c, (ci + 1) * c)
        qts, kks, pieces = [], [], []
        for cols in head_cols:
            f_half = 0.5 * (1.0 - lb_all[:, cols])
            f_mid = lb_all[:, cols] + f_half
            z = f_ref[rows, cols].astype(F32)
            qr = q_ref[rows, cols].astype(F32)
            ft = f_half * jnp.tanh(0.5 * z)
            logf = jnp.log2(f_mid + ft)
            kks.append(f_half - ft)
            qh = (0.5 * HGRN_DK ** -0.5) * qr
            qts.append(qh + qh * jnp.tanh(0.5 * qr))
            hi = logf.astype(BF16)
            pieces.append((hi, (logf - hi.astype(F32)).astype(BF16)))
        split = jnp.concatenate(
            [jnp.concatenate([p[0] for p in pieces], axis=1),
             jnp.concatenate([p[1] for p in pieces], axis=1)], axis=0)
        return qts, kks, _dot(w_ref[...], split)

    def coarse_exponent(b, m):
        pieces = []
        for lo in range(0, c, 2 * m):
            ref = b[lo + m - 1:lo + m, :]
            pieces += [ref - b[lo:lo + m, :], b[lo + m:lo + 2 * m, :] - ref]
        return jnp.concatenate(pieces, axis=0)

    def intra(ci, qts, kks, xs):
        parts = []
        for h, cols in enumerate(head_cols):
            qt, kk = qts[h].astype(BF16), kks[h].astype(BF16)
            b = xs[n_fine * c:(n_fine + 1) * c, cols]
            a = m_ref[nl * c:(nl + 1) * c, :] * _dot_nt(qt, kk)
            for li, m in enumerate(HGRN_LEVELS):
                x = xs[li * c:(li + 1) * c, cols] if li < n_fine else coarse_exponent(b, m)
                e = jnp.exp2(x).astype(BF16)
                a = a + m_ref[li * c:(li + 1) * c, :] * _dot_nt(qt * e, kk * e)
            eb = jnp.exp2(b)
            el = jnp.exp2(b[c - 1:c, :] - b)
            v = i_ref[ci * c:(ci + 1) * c, cols]
            parts.append((qt * eb.astype(BF16), _dot(a.astype(BF16), v), eb[c - 1:c, :],
                          _dot(v.astype(F32).T.astype(BF16), kk * el.astype(BF16))))
        return parts

    def carry_state(ci, parts, states):
        rows = slice(ci * c, (ci + 1) * c)
        new_states = []
        for h, cols in enumerate(head_cols):
            q_decayed, o_intra, chunk_decay, kv = parts[h]
            st = states[h]
            o = _dot_nt(q_decayed, st.astype(BF16)) + o_intra
            new_states.append(st * chunk_decay + kv)
            gate = g_ref[rows, cols].astype(F32)
            y = o * _rms_scale(o) * gn
            gh = 0.5 * gate
            y = y * (gh + gh * jnp.tanh(gh))
            o_ref[rows, cols] = y.astype(o_ref.dtype)
        return new_states

    states = [st_ref[h] for h in range(heads)]
    g_vals, i_vals = {0: gates(0)}, {}
    for k in range(n_chunks + 1):
        if k + 1 < n_chunks:
            g_vals[k + 1] = gates(k + 1)
        if k < n_chunks:
            i_vals[k] = intra(k, *g_vals.pop(k))
        if k >= 1:
            states = carry_state(k - 1, i_vals.pop(k - 1), states)
    for h in range(heads):
        st_ref[h] = states[h]


def _hgrn(proj, lb_logits, gnorm, layer, batch, seq, rows=1024):
    t = batch * seq
    steps = seq // rows
    hp = HGRN_HEADS_PER_STEP
    width = hp * HGRN_DK
    q_blk, f_blk, i_blk, g_blk = (_block_index(PROJ_OFFSETS[s], width) for s in (3, 4, 5, 6))
    rng, masks = _hgrn_constants()
    rng = jnp.asarray(rng, dtype=BF16)
    masks = jnp.asarray(masks, dtype=F32)
    const = lambda a: pl.BlockSpec(a.shape, lambda b, h, s: (0, 0))
    row_spec = lambda base: pl.BlockSpec(
        (rows, width), lambda b, h, s, base=base: (b * steps + s, base + h))
    return pl.pallas_call(
        functools.partial(_hgrn_kernel, layer=layer),
        grid=(batch, HGRN_HEADS // hp, steps),
        in_specs=[pl.BlockSpec((lb_logits.shape[0], width), lambda b, h, s: (0, h)),
                  pl.BlockSpec((1, HGRN_DV), lambda b, h, s: (0, 0)),
                  const(rng), const(masks),
                  row_spec(q_blk), row_spec(f_blk), row_spec(i_blk), row_spec(g_blk)],
        out_specs=pl.BlockSpec((rows, width), lambda b, h, s: (b * steps + s, h)),
        out_shape=jax.ShapeDtypeStruct((t, HGRN_WIDTH), BF16),
        scratch_shapes=[pltpu.VMEM((hp, HGRN_DV, HGRN_DK), F32)],
        compiler_params=pltpu.CompilerParams(
            dimension_semantics=("arbitrary", "arbitrary", "arbitrary")),
        name="hgrn2",
    )(lb_logits, gnorm, rng, masks, proj, proj, proj, proj)


def _merge_kernel(x_ref, ya_ref, yh_ref, wa_ref, wh_ref, wo_ref, gp_ref, *rest):
    n_tiles = (len(rest) - 2) // 2
    ga_refs, gb_refs = rest[:n_tiles], rest[n_tiles:2 * n_tiles]
    o_ref, merged_ref = rest[2 * n_tiles:]
    ya, yh = ya_ref[...], yh_ref[...]
    tile_cols = [slice(c * MXU_SUBTILE, (c + 1) * MXU_SUBTILE) for c in range(n_tiles)]
    for cols, ga_ref, gb_ref in zip(tile_cols, ga_refs, gb_refs):
        merged_ref[:, cols] = (
            _sigmoid(ga_ref[...].astype(F32)) * _dot(ya, wa_ref[:, cols])
            + _sigmoid(gb_ref[...].astype(F32)) * _dot(yh, wh_ref[:, cols])).astype(BF16)
    merged = merged_ref[...]
    for cols in tile_cols:
        o_ref[:, cols] = _dot(merged, wo_ref[:, cols])
    m2 = o_ref[...]
    o_ref[...] = x_ref[...] + m2 * _rms_scale(m2) * gp_ref[...]


def _merge(x2, y_attn, y_hgrn, proj, wa, wh, wo, g_post, tm=512):
    t, d = x2.shape
    n_tiles = d // MXU_SUBTILE
    ga_blk, gb_blk = (_block_index(PROJ_OFFSETS[s], MXU_SUBTILE) for s in (7, 8))
    const = lambda shape: pl.BlockSpec(shape, lambda i: (0, 0), pipeline_mode=pl.Buffered(1))
    gate_specs = [pl.BlockSpec((tm, MXU_SUBTILE), lambda i, blk=base + c: (i, blk))
                  for base in (ga_blk, gb_blk) for c in range(n_tiles)]
    return pl.pallas_call(
        _merge_kernel,
        grid=(t // tm,),
        in_specs=[pl.BlockSpec((tm, d), lambda i: (i, 0)),
                  pl.BlockSpec((tm, ATTN_Q_WIDTH), lambda i: (i, 0)),
                  pl.BlockSpec((tm, HGRN_WIDTH), lambda i: (i, 0)),
                  const(wa.shape), const(wh.shape), const(wo.shape), const((1, d))] + gate_specs,
        out_specs=pl.BlockSpec((tm, d), lambda i: (i, 0)),
        out_shape=jax.ShapeDtypeStruct((t, d), F32),
        scratch_shapes=[pltpu.VMEM((tm, d), BF16)],
        compiler_params=pltpu.CompilerParams(
            dimension_semantics=("arbitrary",), vmem_limit_bytes=VMEM_LIMIT_BYTES),
        name="merge_out",
    )(x2, y_attn, y_hgrn, wa, wh, wo, g_post, *([proj] * (2 * n_tiles)))


def _ffn_kernel(x_ref, gpre_ref, wgu_hbm, wd_hbm, gpost_ref, o_ref, h_ref, wg_buf, wu_buf, wd_buf,
                sem):
    i = pl.program_id(0)
    tf = wg_buf.shape[2]
    n_tiles = D_FF // tf
    ring = wg_buf.shape[0] - 1
    slot_of = lambda j: ring if j == 0 else (j - 1) % ring

    def weight_copies(j):
        slot = slot_of(j)
        return (pltpu.make_async_copy(wgu_hbm.at[:, pl.ds(j * tf, tf)], wg_buf.at[slot],
                                      sem.at[0, slot]),
                pltpu.make_async_copy(wgu_hbm.at[:, pl.ds(D_FF + j * tf, tf)], wu_buf.at[slot],
                                      sem.at[1, slot]),
                pltpu.make_async_copy(wd_hbm.at[pl.ds(j * tf, tf), :], wd_buf.at[slot],
                                      sem.at[2, slot]))

    @pl.when(i == 0)
    def _():
        for copy in weight_copies(0):
            copy.start()

    x = x_ref[...]
    h_ref[...] = (x * _rms_scale(x) * gpre_ref[...]).astype(BF16)
    h = h_ref[...]
    for copy in weight_copies(1):
        copy.start()
    for j in range(n_tiles):
        for copy in weight_copies(j):
            copy.wait()
        slot = slot_of(j)
        gate = _dot(h, wg_buf[slot])
        up = _dot(h, wu_buf[slot])
        if j + 2 < n_tiles:
            for copy in weight_copies(j + 2):
                copy.start()
        if j == 1:
            for copy in weight_copies(0):
                copy.start()
        act = (gate * _sigmoid(gate) * up).astype(BF16)
        for c0 in range(0, o_ref.shape[1], MXU_SUBTILE):
            cols = slice(c0, c0 + MXU_SUBTILE)
            down = _dot(act, wd_buf[slot, :, cols])
            o_ref[:, cols] = down if j == 0 else o_ref[:, cols] + down
    y = o_ref[...]
    o_ref[...] = x_ref[...] + y * _rms_scale(y) * gpost_ref[...]

    @pl.when(i == pl.num_programs(0) - 1)
    def _():
        for copy in weight_copies(0):
            copy.wait()


def _ffn(x1, g_pre, w_gate_up, w_down, g_post, tm=512, tf=512):
    t, d = x1.shape
    slots = 4
    return pl.pallas_call(
        _ffn_kernel,
        grid=(t // tm,),
        in_specs=[pl.BlockSpec((tm, d), lambda i: (i, 0)),
                  pl.BlockSpec((1, d), lambda i: (0, 0)),
                  pl.BlockSpec(memory_space=pl.ANY),
                  pl.BlockSpec(memory_space=pl.ANY),
                  pl.BlockSpec((1, d), lambda i: (0, 0))],
        out_specs=pl.BlockSpec((tm, d), lambda i: (i, 0)),
        out_shape=jax.ShapeDtypeStruct((t, d), F32),
        scratch_shapes=[pltpu.VMEM((tm, d), BF16),
                        pltpu.VMEM((slots, d, tf), BF16), pltpu.VMEM((slots, d, tf), BF16),
                        pltpu.VMEM((slots, tf, d), BF16), pltpu.SemaphoreType.DMA((3, slots))],
        compiler_params=pltpu.CompilerParams(
            dimension_semantics=("arbitrary",), vmem_limit_bytes=VMEM_LIMIT_BYTES),
        name="swiglu_ffn",
    )(x1, g_pre, w_gate_up, w_down, g_post)


def _ple_kernel(x_ref, p_ref, gpre_ref, wg_ref, wp_ref, gpost_ref, o_ref):
    x = x_ref[...]
    h = (x * _rms_scale(x) * gpre_ref[...]).astype(BF16)
    p16 = p_ref[...].astype(BF16)
    for c0 in range(0, o_ref.shape[1], MXU_SUBTILE):
        cols = slice(c0, c0 + MXU_SUBTILE)
        gate = _sigmoid(_dot(h, wg_ref[:, cols]))
        o_ref[:, cols] = _dot(p16, wp_ref[:, cols]) * gate
    e = o_ref[...]
    o_ref[...] = x_ref[...] + e * _rms_scale(e) * gpost_ref[...]


def _ple(x2, p2, g_pre, w_gate, w_proj, g_post, tm=512):
    t, d = x2.shape
    const = lambda shape: pl.BlockSpec(shape, lambda i: (0, 0), pipeline_mode=pl.Buffered(1))
    return pl.pallas_call(
        _ple_kernel,
        grid=(t // tm,),
        in_specs=[pl.BlockSpec((tm, d), lambda i: (i, 0)),
                  pl.BlockSpec((tm, PLE_DIM), lambda i: (i, 0)),
                  const((1, d)), const(w_gate.shape), const(w_proj.shape), const((1, d))],
        out_specs=pl.BlockSpec((tm, d), lambda i: (i, 0)),
        out_shape=jax.ShapeDtypeStruct((t, d), F32),
        compiler_params=pltpu.CompilerParams(
            dimension_semantics=("arbitrary",), vmem_limit_bytes=VMEM_LIMIT_BYTES),
        name="ple",
    )(x2, p2, g_pre, w_gate, w_proj, g_post)


def kernel(x, p, positions, g_mix_pre, w_in, attn_sinks, hgrn_lb_logits, hgrn_gnorm,
           w_attn_branch, w_hgrn_branch, w_out, g_mix_post, g_ffn_pre, w_gate_up, w_down,
           g_ffn_post, g_ple_pre, w_ple_gate, w_ple_proj, g_ple_post):
    batch, seq, d = x.shape
    t = batch * seq
    depth = w_in.shape[0]
    row = lambda v: v.reshape(1, -1)

    xs = x.reshape(t, d)
    for layer in range(depth):
        proj = _inproj(xs, row(g_mix_pre[layer]), w_in[layer].astype(BF16))
        y_attn, (wa, wh, wo, wgu, wd, wpg) = _attention(
            proj, positions, attn_sinks[layer],
            (w_attn_branch[layer], w_hgrn_branch[layer], w_out[layer], w_gate_up[layer],
             w_down[layer], w_ple_gate[layer]), batch, seq)
        y_hgrn = _hgrn(proj, hgrn_lb_logits, row(hgrn_gnorm[layer]), layer, batch, seq)
        xs = _merge(xs, y_attn, y_hgrn, proj, wa, wh, wo, row(g_mix_post[layer]))
        xs = _ffn(xs, row(g_ffn_pre[layer]), wgu, wd, row(g_ffn_post[layer]))
        xs = _ple(xs, p[layer].reshape(t, PLE_DIM), row(g_ple_pre[layer]), wpg,
                  w_ple_proj[layer].astype(BF16), row(g_ple_post[layer]))
    return xs.reshape(batch, seq, d)
```

```python
import functools
import math

import numpy as np
import jax
import jax.numpy as jnp
from jax import lax
from jax.experimental import pallas as pl
from jax.experimental.pallas import tpu as pltpu

D_MODEL = 2048
PLE_DIM = 256
ATTN_HEADS = 16
ATTN_KV_HEADS = 4
ATTN_HEAD_DIM = 64
ATTN_Q_WIDTH = ATTN_HEADS * ATTN_HEAD_DIM
ATTN_KV_WIDTH = ATTN_KV_HEADS * ATTN_HEAD_DIM
ATTN_BLOCK = 128
ATTN_BLOCKS_PER_STEP = 4
ROPE_THETA = 10000.0
HGRN_HEADS = 8
HGRN_DK = 128
HGRN_DV = 128
HGRN_WIDTH = HGRN_HEADS * HGRN_DK
D_FF = 5632
RMS_EPS = 1e-6
IN_SIZES = (ATTN_Q_WIDTH, ATTN_KV_WIDTH, ATTN_KV_WIDTH,
            HGRN_WIDTH, HGRN_WIDTH, HGRN_WIDTH, HGRN_WIDTH, D_MODEL, D_MODEL)
IN_WIDTH = sum(IN_SIZES)
PROJ_OFFSETS = tuple(int(v) for v in np.cumsum((0,) + IN_SIZES[:-1]))

LANES = 128
SUBLANES = 8
HGRN_CHUNK = 128
HGRN_LEVELS = (1, 2, 4, 8, 16, 32, 64)
HGRN_HEADS_PER_STEP = 4
MXU_SUBTILE = 512
FFN_TILE = 512
FFN_DOWN_SUBTILE = 1024
INPROJ_SUBTILE = 1024
VMEM_LIMIT_BYTES = 58 * 1024 * 1024
LOG2E = math.log2(math.e)
MASKED_SCORE = -1e30

BF16 = jnp.bfloat16
F32 = jnp.float32


def _rms_scale(v):
    return lax.rsqrt(jnp.mean(v * v, axis=-1, keepdims=True) + RMS_EPS)


def _sigmoid(v):
    return 0.5 * jnp.tanh(0.5 * v) + 0.5


def _dot(a, b):
    return jnp.dot(a, b, preferred_element_type=F32)


def _block_index(offset, width):
    assert offset % width == 0, (offset, width)
    return offset // width


def _dot_nt(a, b):
    return lax.dot_general(a, b, (((1,), (1,)), ((), ())), preferred_element_type=F32)


def _inproj_kernel(x_ref, g_ref, w_ref, o_ref, h_ref):
    @pl.when(pl.program_id(1) == 0)
    def _():
        x = x_ref[...]
        h_ref[...] = (x * _rms_scale(x) * g_ref[...]).astype(BF16)

    for c0 in range(0, o_ref.shape[1], INPROJ_SUBTILE):
        cols = slice(c0, min(c0 + INPROJ_SUBTILE, o_ref.shape[1]))
        o_ref[:, cols] = _dot(h_ref[...], w_ref[:, cols]).astype(o_ref.dtype)


def _inproj(x2, g, w, tm=1024, tn=2432):
    t, d = x2.shape
    n = w.shape[1]
    assert t % tm == 0 and n % tn == 0
    return pl.pallas_call(
        _inproj_kernel,
        grid=(t // tm, n // tn),
        in_specs=[pl.BlockSpec((tm, d), lambda i, j: (i, 0)),
                  pl.BlockSpec((1, d), lambda i, j: (0, 0)),
                  pl.BlockSpec((d, tn), lambda i, j: (0, j))],
        out_specs=pl.BlockSpec((tm, tn), lambda i, j: (i, j)),
        out_shape=jax.ShapeDtypeStruct((t, n), BF16),
        scratch_shapes=[pltpu.VMEM((tm, d), BF16)],
        compiler_params=pltpu.CompilerParams(
            dimension_semantics=("arbitrary", "arbitrary"),
            vmem_limit_bytes=VMEM_LIMIT_BYTES),
        name="in_proj",
    )(x2, g, w)


def _attn_kernel(sink_ref, pos_ref, invf_ref, ctile_ref, stile_ref, q_ref, k_ref, v_ref, *rest,
                 tile_groups):
    n_w = (len(rest) - 3) // 2
    o_ref, (kz_ref, vz_ref) = rest[n_w], rest[-2:]
    for w_ref, w16_ref, groups in zip(rest[:n_w], rest[n_w + 1:2 * n_w + 1], tile_groups):
        if groups is None:
            w16_ref[...] = w_ref[...].astype(w16_ref.dtype)
            continue
        tiles = w16_ref.shape[0]
        width = w16_ref.shape[2] // groups
        for tile in range(tiles):
            for g in range(groups):
                src = (g * tiles + tile) * width
                w16_ref[tile, :, g * width:(g + 1) * width] = (
                    w_ref[:, src:src + width].astype(w16_ref.dtype))

    blk = ATTN_BLOCK
    nblk = q_ref.shape[0] // blk
    n = pl.program_id(1)
    carry_in = nblk + n % 2
    carry_out = nblk + (n + 1) % 2

    @pl.when(n == 0)
    def _():
        kz_ref[nblk] = jnp.zeros(kz_ref.shape[1:], kz_ref.dtype)
        vz_ref[nblk] = jnp.zeros(vz_ref.shape[1:], vz_ref.dtype)

    ang_t = invf_ref[...] * pos_ref[0]
    q_scale = LOG2E * ATTN_HEAD_DIM ** -0.5

    def spread(tab_t, tile_ref):
        p1 = tab_t.astype(BF16)
        r1 = tab_t - p1.astype(F32)
        p2 = r1.astype(BF16)
        p3 = (r1 - p2.astype(F32)).astype(BF16)
        stacked = jnp.concatenate([p1, p2, p3, jnp.zeros_like(p1)], axis=0)
        return _dot(stacked.astype(F32).T.astype(BF16), tile_ref[...])

    cos_t = jnp.cos(ang_t)
    sin_t = jnp.sin(ang_t)
    cos = spread(cos_t, ctile_ref)
    sin_signed = spread(sin_t, stile_ref)
    cos_q = spread(cos_t * q_scale, ctile_ref)
    sin_q = spread(sin_t * q_scale, stile_ref)

    lane = lax.broadcasted_iota(jnp.int32, (blk, LANES), 1)
    row = lax.broadcasted_iota(jnp.int32, (blk, LANES), 0)
    first_half = (lane & (ATTN_HEAD_DIM // 2)) == 0
    low_head = lane < ATTN_HEAD_DIM
    from_prev = lane > row

    def rope(t, j, c, s):
        rows = slice(j * blk, (j + 1) * blk)
        partner = jnp.where(first_half,
                            pltpu.roll(t, LANES - ATTN_HEAD_DIM // 2, 1),
                            pltpu.roll(t, ATTN_HEAD_DIM // 2, 1))
        return t * c[rows] + partner * s[rows]

    def place(src, dst, j, c):
        slots = (j, carry_out) if j == nblk - 1 else (j,)
        for e in range(2):
            own = jnp.where(low_head if e == 0 else jnp.logical_not(low_head), src, 0.0)
            other = pltpu.roll(own, ATTN_HEAD_DIM, 1)
            lo, hi = (own, other) if e == 0 else (other, own)
            for slot in slots:
                dst[slot, 2 * c + e, 0:blk, 0:LANES] = lo.astype(BF16)
                dst[slot, 2 * c + e, blk:2 * blk, 0:LANES] = hi.astype(BF16)

    def place_keys(j):
        rows = slice(j * blk, (j + 1) * blk)
        for c in range(ATTN_KV_WIDTH // LANES):
            place(rope(k_ref[rows, c * LANES:(c + 1) * LANES].astype(F32), j, cos, sin_signed),
                  kz_ref, j, c)

    def place_values(j):
        rows = slice(j * blk, (j + 1) * blk)
        ones_lo = low_head.astype(BF16)
        for slot in ((j, carry_out) if j == nblk - 1 else (j,)):
            for hk in range(ATTN_KV_HEADS):
                vz_ref[slot, hk, 0:blk, LANES:2 * LANES] = ones_lo
                vz_ref[slot, hk, blk:2 * blk, LANES:2 * LANES] = 1 - ones_lo
        for c in range(ATTN_KV_WIDTH // LANES):
            place(v_ref[rows, c * LANES:(c + 1) * LANES].astype(F32), vz_ref, j, c)

    first_penalty = jnp.where(n > 0, 0.0, MASKED_SCORE)
    items = [(j, hk) for j in range(nblk) for hk in range(ATTN_KV_HEADS)]

    def scores(j, hk):
        rows = slice(j * blk, (j + 1) * blk)
        q = jnp.concatenate(
            [rope(q_ref[rows, g * LANES:(g + 1) * LANES].astype(F32), j, cos_q, sin_q).astype(BF16)
             for g in (2 * hk, 2 * hk + 1)], axis=0)
        prv = carry_in if j == 0 else j - 1
        return _dot_nt(q, kz_ref[prv, hk]), _dot_nt(q, kz_ref[j, hk])

    def fold_and_max(j, hk, s_prev, s_cur):
        folded = []
        for rh in range(2):
            rows = slice(rh * blk, (rh + 1) * blk)
            for ch in range(2):
                sink = sink_ref[4 * hk + 2 * rh + ch] * LOG2E
                cols = slice(ch * blk, (ch + 1) * blk)
                sp = s_prev[rows, cols] + first_penalty if j == 0 else s_prev[rows, cols]
                s = jnp.where(from_prev, sp, s_cur[rows, cols])
                folded.append((s, jnp.maximum(jnp.max(s, axis=-1, keepdims=True), sink), sink))
        return folded

    def weigh(j, hk, folded):
        ps, sink_terms = [], []
        for s, m, sink in folded:
            ps.append(jnp.exp2(s - m))
            sink_terms.append(jnp.exp2(sink - m))
        stack = lambda tiles: jnp.concatenate(
            [jnp.concatenate(tiles[0:2], axis=1), jnp.concatenate(tiles[2:4], axis=1)], axis=0)
        p_prev = stack([jnp.where(from_prev, p, 0.0).astype(BF16) for p in ps])
        p_cur = stack([jnp.where(from_prev, 0.0, p).astype(BF16) for p in ps])
        prv = carry_in if j == 0 else j - 1
        acc = _dot(p_prev, vz_ref[prv, hk]) + _dot(p_cur, vz_ref[j, hk])
        return acc, sink_terms

    def finish(j, hk, acc, sink_terms):
        for rh in range(2):
            rows = slice(rh * blk, (rh + 1) * blk)
            g = 2 * hk + rh
            denom = acc[rows, LANES:] + jnp.where(low_head, sink_terms[2 * rh],
                                                  sink_terms[2 * rh + 1])
            o_ref[j * blk:(j + 1) * blk, g * LANES:(g + 1) * LANES] = (
                acc[rows, :LANES] / denom).astype(o_ref.dtype)

    place_keys(0)
    s_vals, f_vals, w_vals = {}, {}, {}
    for k in range(len(items) + 3):
        if k < len(items):
            j, hk = items[k]
            s_vals[k] = scores(j, hk)
            if hk == 0:
                place_values(j)
            if hk == 1 and j + 1 < nblk:
                place_keys(j + 1)
        if 0 <= k - 1 < len(items):
            f_vals[k - 1] = fold_and_max(*items[k - 1], *s_vals.pop(k - 1))
        if 0 <= k - 2 < len(items):
            w_vals[k - 2] = weigh(*items[k - 2], f_vals.pop(k - 2))
        if 0 <= k - 3 < len(items):
            finish(*items[k - 3], *w_vals.pop(k - 3))


def _rope_constants():
    half = ATTN_HEAD_DIM // 2
    inv_freq = ROPE_THETA ** (-np.arange(half, dtype=np.float32) / half)
    k = np.arange(LANES)[:, None]
    lane = np.arange(LANES)[None, :]
    hit = ((k < 3 * half) & (k % half == lane % half)).astype(np.float32)
    sign = np.where(lane % ATTN_HEAD_DIM < half, -1.0, 1.0).astype(np.float32)
    return inv_freq.reshape(half, 1), hit, hit * sign


def _attention(proj, positions, sinks, weights, tilings, batch, seq):
    blk = ATTN_BLOCK
    rows = ATTN_BLOCKS_PER_STEP * blk
    assert seq % rows == 0
    nb = seq // rows
    t = batch * seq
    steps = batch * nb
    w_specs, w16_specs, w16_shapes = [], [], []
    for w, tiling in zip(weights, tilings):
        assert w.shape[0] % (steps * 2 * SUBLANES) == 0, w.shape
        blk_rows = w.shape[0] // steps
        w_specs.append(pl.BlockSpec((blk_rows, w.shape[1]), lambda b, n: (b * nb + n, 0)))
        if tiling is None:
            w16_specs.append(w_specs[-1])
            w16_shapes.append(jax.ShapeDtypeStruct(w.shape, BF16))
        else:
            width = tiling[0] * tiling[1]
            tiles = w.shape[1] // width
            w16_specs.append(pl.BlockSpec((tiles, blk_rows, width), lambda b, n: (0, b * nb + n, 0)))
            w16_shapes.append(jax.ShapeDtypeStruct((tiles, w.shape[0], width), BF16))
    tile_groups = tuple(None if tiling is None else tiling[1] for tiling in tilings)
    inv_freq, ctile, stile = _rope_constants()
    pos_rows = positions.astype(F32).reshape(t // rows, 1, rows)
    const = lambda a: pl.BlockSpec(a.shape, lambda b, n: (0, 0))
    q_blk, k_blk, v_blk = (_block_index(PROJ_OFFSETS[0], ATTN_Q_WIDTH),
                           _block_index(PROJ_OFFSETS[1], ATTN_KV_WIDTH),
                           _block_index(PROJ_OFFSETS[2], ATTN_KV_WIDTH))
    slots = ATTN_BLOCKS_PER_STEP + 2
    k_scratch = pltpu.VMEM((slots, ATTN_KV_HEADS, 2 * blk, LANES), BF16)
    v_scratch = pltpu.VMEM((slots, ATTN_KV_HEADS, 2 * blk, 2 * LANES), BF16)
    outs = pl.pallas_call(
        functools.partial(_attn_kernel, tile_groups=tile_groups),
        grid=(batch, nb),
        in_specs=[pl.BlockSpec(memory_space=pltpu.SMEM),
                  pl.BlockSpec((1, 1, rows), lambda b, n: (b * nb + n, 0, 0)),
                  const(inv_freq), const(ctile), const(stile),
                  pl.BlockSpec((rows, ATTN_Q_WIDTH), lambda b, n: (b * nb + n, q_blk)),
                  pl.BlockSpec((rows, ATTN_KV_WIDTH), lambda b, n: (b * nb + n, k_blk)),
                  pl.BlockSpec((rows, ATTN_KV_WIDTH), lambda b, n: (b * nb + n, v_blk))] + w_specs,
        out_specs=[pl.BlockSpec((rows, ATTN_Q_WIDTH), lambda b, n: (b * nb + n, 0))] + w16_specs,
        out_shape=[jax.ShapeDtypeStruct((t, ATTN_Q_WIDTH), BF16)] + w16_shapes,
        scratch_shapes=[k_scratch, v_scratch],
        compiler_params=pltpu.CompilerParams(
            dimension_semantics=("arbitrary", "arbitrary"),
            vmem_limit_bytes=VMEM_LIMIT_BYTES),
        name="swa_attention",
    )(sinks, pos_rows, jnp.asarray(inv_freq), jnp.asarray(ctile, dtype=BF16),
      jnp.asarray(stile, dtype=BF16), proj, proj, proj, *weights)
    return outs[0], outs[1:]


def _hgrn_constants():
    c = HGRN_CHUNK
    u = np.arange(c)[None, :]
    t = np.arange(c)[:, None]
    ranges, masks = [], []
    for m in HGRN_LEVELS:
        upper = (t & m) != 0
        start = (t // m) * m
        q_rng = (u >= start) & (u <= t)
        k_rng = (u > t) & (u <= start + m - 1)
        if m < SUBLANES:
            ranges.append(np.where(upper, q_rng, k_rng))
        masks.append(upper & ((u & m) == 0) & ((t // (2 * m)) == (u // (2 * m))))
    ranges.append(u <= t)
    masks.append(u == t)
    rng = np.concatenate(ranges, axis=0).astype(np.float32)
    return np.concatenate([rng, rng], axis=1), np.concatenate(masks, axis=0).astype(np.float32)


def _hgrn_kernel(lbl_ref, gn_ref, w_ref, m_ref, q_ref, f_ref, i_ref, g_ref, o_ref, st_ref,
                 *, layer):
    c = HGRN_CHUNK
    nl = len(HGRN_LEVELS)
    n_fine = sum(m < SUBLANES for m in HGRN_LEVELS)
    heads = q_ref.shape[1] // HGRN_DK
    n_chunks = q_ref.shape[0] // c

    @pl.when(pl.program_id(2) == 0)
    def _():
        st_ref[...] = jnp.zeros_like(st_ref)

    lg = lbl_ref[...]
    eg = jnp.exp(lg - jnp.max(lg, axis=0, keepdims=True))
    lb_all = (jnp.sum(eg[0:layer + 1, :], axis=0, keepdims=True)
              / jnp.sum(eg, axis=0, keepdims=True))
    gn = gn_ref[...]

    head_cols = [slice(h * HGRN_DK, (h + 1) * HGRN_DK) for h in range(heads)]

    def gates(ci):
        rows = slice(ci * c, (ci + 1) * c)
        qts, kks, pieces = [], [], []
        for cols in head_cols:
            f_half = 0.5 * (1.0 - lb_all[:, cols])
            f_mid = lb_all[:, cols] + f_half
            z = f_ref[rows, cols].astype(F32)
            qr = q_ref[rows, cols].astype(F32)
            ft = f_half * jnp.tanh(0.5 * z)
            logf = jnp.log2(f_mid + ft)
            kks.append(f_half - ft)
            qh = (0.5 * HGRN_DK ** -0.5) * qr
            qts.append(qh + qh * jnp.tanh(0.5 * qr))
            hi = logf.astype(BF16)
            pieces.append((hi, (logf - hi.astype(F32)).astype(BF16)))
        split = jnp.concatenate(
            [jnp.concatenate([p[0] for p in pieces], axis=1),
             jnp.concatenate([p[1] for p in pieces], axis=1)], axis=0)
        return qts, kks, _dot(w_ref[...], split)

    def coarse_exponent(b, m):
        pieces = []
        for lo in range(0, c, 2 * m):
            ref = b[lo + m - 1:lo + m, :]
            pieces += [ref - b[lo:lo + m, :], b[lo + m:lo + 2 * m, :] - ref]
        return jnp.concatenate(pieces, axis=0)

    def intra(ci, qts, kks, xs):
        parts = []
        for h, cols in enumerate(head_cols):
            qt, kk = qts[h].astype(BF16), kks[h].astype(BF16)
            b = xs[n_fine * c:(n_fine + 1) * c, cols]
            a = m_ref[nl * c:(nl + 1) * c, :] * _dot_nt(qt, kk)
            for li, m in enumerate(HGRN_LEVELS):
                x = xs[li * c:(li + 1) * c, cols] if li < n_fine else coarse_exponent(b, m)
                e = jnp.exp2(x).astype(BF16)
                a = a + m_ref[li * c:(li + 1) * c, :] * _dot_nt(qt * e, kk * e)
            eb = jnp.exp2(b)
            el = jnp.exp2(b[c - 1:c, :] - b)
            v = i_ref[ci * c:(ci + 1) * c, cols]
            parts.append((qt * eb.astype(BF16), _dot(a.astype(BF16), v), eb[c - 1:c, :],
                          _dot(v.astype(F32).T.astype(BF16), kk * el.astype(BF16))))
        return parts

    def carry_state(ci, parts, states):
        rows = slice(ci * c, (ci + 1) * c)
        new_states = []
        for h, cols in enumerate(head_cols):
            q_decayed, o_intra, chunk_decay, kv = parts[h]
            st = states[h]
            o = _dot_nt(q_decayed, st.astype(BF16)) + o_intra
            new_states.append(st * chunk_decay + kv)
            gate = g_ref[rows, cols].astype(F32)
            y = o * _rms_scale(o) * gn
            gh = 0.5 * gate
            y = y * (gh + gh * jnp.tanh(gh))
            o_ref[rows, cols] = y.astype(o_ref.dtype)
        return new_states

    states = [st_ref[h] for h in range(heads)]
    g_vals, i_vals = {0: gates(0)}, {}
    for k in range(n_chunks + 1):
        if k + 1 < n_chunks:
            g_vals[k + 1] = gates(k + 1)
        if k < n_chunks:
            i_vals[k] = intra(k, *g_vals.pop(k))
        if k >= 1:
            states = carry_state(k - 1, i_vals.pop(k - 1), states)
    for h in range(heads):
        st_ref[h] = states[h]


def _hgrn(proj, lb_logits, gnorm, layer, batch, seq, rows=1024):
    t = batch * seq
    steps = seq // rows
    hp = HGRN_HEADS_PER_STEP
    width = hp * HGRN_DK
    q_blk, f_blk, i_blk, g_blk = (_block_index(PROJ_OFFSETS[s], width) for s in (3, 4, 5, 6))
    rng, masks = _hgrn_constants()
    rng = jnp.asarray(rng, dtype=BF16)
    masks = jnp.asarray(masks, dtype=F32)
    const = lambda a: pl.BlockSpec(a.shape, lambda b, h, s: (0, 0))
    row_spec = lambda base: pl.BlockSpec(
        (rows, width), lambda b, h, s, base=base: (b * steps + s, base + h))
    return pl.pallas_call(
        functools.partial(_hgrn_kernel, layer=layer),
        grid=(batch, HGRN_HEADS // hp, steps),
        in_specs=[pl.BlockSpec((lb_logits.shape[0], width), lambda b, h, s: (0, h)),
                  pl.BlockSpec((1, HGRN_DV), lambda b, h, s: (0, 0)),
                  const(rng), const(masks),
                  row_spec(q_blk), row_spec(f_blk), row_spec(i_blk), row_spec(g_blk)],
        out_specs=pl.BlockSpec((rows, width), lambda b, h, s: (b * steps + s, h)),
        out_shape=jax.ShapeDtypeStruct((t, HGRN_WIDTH), BF16),
        scratch_shapes=[pltpu.VMEM((hp, HGRN_DV, HGRN_DK), F32)],
        compiler_params=pltpu.CompilerParams(
            dimension_semantics=("arbitrary", "arbitrary", "arbitrary")),
        name="hgrn2",
    )(lb_logits, gnorm, rng, masks, proj, proj, proj, proj)


def _merge_kernel(x_ref, ya_ref, yh_ref, wa_ref, wh_ref, wo_ref, gp_ref, *rest):
    n_tiles = (len(rest) - 2) // 2
    ga_refs, gb_refs = rest[:n_tiles], rest[n_tiles:2 * n_tiles]
    o_ref, merged_ref = rest[2 * n_tiles:]
    ya, yh = ya_ref[...], yh_ref[...]
    tile_cols = [slice(c * MXU_SUBTILE, (c + 1) * MXU_SUBTILE) for c in range(n_tiles)]
    for cols, ga_ref, gb_ref in zip(tile_cols, ga_refs, gb_refs):
        merged_ref[:, cols] = (
            _sigmoid(ga_ref[...].astype(F32)) * _dot(ya, wa_ref[:, cols])
            + _sigmoid(gb_ref[...].astype(F32)) * _dot(yh, wh_ref[:, cols])).astype(BF16)
    merged = merged_ref[...]
    for cols in tile_cols:
        o_ref[:, cols] = _dot(merged, wo_ref[:, cols])
    m2 = o_ref[...]
    o_ref[...] = x_ref[...] + m2 * _rms_scale(m2) * gp_ref[...]


def _merge(x2, y_attn, y_hgrn, proj, wa, wh, wo, g_post, tm=512):
    t, d = x2.shape
    n_tiles = d // MXU_SUBTILE
    ga_blk, gb_blk = (_block_index(PROJ_OFFSETS[s], MXU_SUBTILE) for s in (7, 8))
    const = lambda shape: pl.BlockSpec(shape, lambda i: (0, 0), pipeline_mode=pl.Buffered(1))
    gate_specs = [pl.BlockSpec((tm, MXU_SUBTILE), lambda i, blk=base + c: (i, blk))
                  for base in (ga_blk, gb_blk) for c in range(n_tiles)]
    return pl.pallas_call(
        _merge_kernel,
        grid=(t // tm,),
        in_specs=[pl.BlockSpec((tm, d), lambda i: (i, 0)),
                  pl.BlockSpec((tm, ATTN_Q_WIDTH), lambda i: (i, 0)),
                  pl.BlockSpec((tm, HGRN_WIDTH), lambda i: (i, 0)),
                  const(wa.shape), const(wh.shape), const(wo.shape), const((1, d))] + gate_specs,
        out_specs=pl.BlockSpec((tm, d), lambda i: (i, 0)),
        out_shape=jax.ShapeDtypeStruct((t, d), F32),
        scratch_shapes=[pltpu.VMEM((tm, d), BF16)],
        compiler_params=pltpu.CompilerParams(
            dimension_semantics=("arbitrary",), vmem_limit_bytes=VMEM_LIMIT_BYTES),
        name="merge_out",
    )(x2, y_attn, y_hgrn, wa, wh, wo, g_post, *([proj] * (2 * n_tiles)))


def _ffn_kernel(x_ref, gpre_ref, wgu_hbm, wd_hbm, gpost_ref, o_ref, h_ref, wgu_buf, wd_buf, sem):
    i = pl.program_id(0)
    tf = wd_buf.shape[1]
    n_tiles = D_FF // tf
    ring = wgu_buf.shape[0] - 1
    slot_of = lambda j: ring if j == 0 else (j - 1) % ring

    def weight_copies(j):
        slot = slot_of(j)
        return (pltpu.make_async_copy(wgu_hbm.at[j], wgu_buf.at[slot], sem.at[0, slot]),
                pltpu.make_async_copy(wd_hbm.at[pl.ds(j * tf, tf), :], wd_buf.at[slot],
                                      sem.at[1, slot]))

    @pl.when(i == 0)
    def _():
        for copy in weight_copies(0):
            copy.start()

    x = x_ref[...]
    h_ref[...] = (x * _rms_scale(x) * gpre_ref[...]).astype(BF16)
    h = h_ref[...]
    for copy in weight_copies(1):
        copy.start()
    for j in range(n_tiles):
        for copy in weight_copies(j):
            copy.wait()
        slot = slot_of(j)
        gate_up = _dot(h, wgu_buf[slot])
        gate, up = gate_up[:, :tf], gate_up[:, tf:]
        if j + 2 < n_tiles:
            for copy in weight_copies(j + 2):
                copy.start()
        if j == 1:
            for copy in weight_copies(0):
                copy.start()
        act = (gate * _sigmoid(gate) * up).astype(BF16)
        for c0 in range(0, o_ref.shape[1], FFN_DOWN_SUBTILE):
            cols = slice(c0, c0 + FFN_DOWN_SUBTILE)
            down = _dot(act, wd_buf[slot, :, cols])
            o_ref[:, cols] = down if j == 0 else o_ref[:, cols] + down
    y = o_ref[...]
    o_ref[...] = x_ref[...] + y * _rms_scale(y) * gpost_ref[...]

    @pl.when(i == pl.num_programs(0) - 1)
    def _():
        for copy in weight_copies(0):
            copy.wait()


def _ffn(x1, g_pre, w_gate_up, w_down, g_post, tm=512):
    t, d = x1.shape
    tf = FFN_TILE
    assert w_gate_up.shape == (D_FF // tf, d, 2 * tf)
    slots = 4
    return pl.pallas_call(
        _ffn_kernel,
        grid=(t // tm,),
        in_specs=[pl.BlockSpec((tm, d), lambda i: (i, 0)),
                  pl.BlockSpec((1, d), lambda i: (0, 0)),
                  pl.BlockSpec(memory_space=pl.ANY),
                  pl.BlockSpec(memory_space=pl.ANY),
                  pl.BlockSpec((1, d), lambda i: (0, 0))],
        out_specs=pl.BlockSpec((tm, d), lambda i: (i, 0)),
        out_shape=jax.ShapeDtypeStruct((t, d), F32),
        scratch_shapes=[pltpu.VMEM((tm, d), BF16),
                        pltpu.VMEM((slots, d, 2 * tf), BF16), pltpu.VMEM((slots, tf, d), BF16),
                        pltpu.SemaphoreType.DMA((2, slots))],
        compiler_params=pltpu.CompilerParams(
            dimension_semantics=("arbitrary",), vmem_limit_bytes=VMEM_LIMIT_BYTES),
        name="swiglu_ffn",
    )(x1, g_pre, w_gate_up, w_down, g_post)


def _ple_kernel(x_ref, p_ref, gpre_ref, wg_ref, wp_ref, gpost_ref, o_ref):
    x = x_ref[...]
    h = (x * _rms_scale(x) * gpre_ref[...]).astype(BF16)
    p16 = p_ref[...].astype(BF16)
    for c0 in range(0, o_ref.shape[1], MXU_SUBTILE):
        cols = slice(c0, c0 + MXU_SUBTILE)
        gate = _sigmoid(_dot(h, wg_ref[:, cols]))
        o_ref[:, cols] = _dot(p16, wp_ref[:, cols]) * gate
    e = o_ref[...]
    o_ref[...] = x_ref[...] + e * _rms_scale(e) * gpost_ref[...]


def _ple(x2, p2, g_pre, w_gate, w_proj, g_post, tm=512):
    t, d = x2.shape
    const = lambda shape: pl.BlockSpec(shape, lambda i: (0, 0), pipeline_mode=pl.Buffered(1))
    return pl.pallas_call(
        _ple_kernel,
        grid=(t // tm,),
        in_specs=[pl.BlockSpec((tm, d), lambda i: (i, 0)),
                  pl.BlockSpec((tm, PLE_DIM), lambda i: (i, 0)),
                  const((1, d)), const(w_gate.shape), const(w_proj.shape), const((1, d))],
        out_specs=pl.BlockSpec((tm, d), lambda i: (i, 0)),
        out_shape=jax.ShapeDtypeStruct((t, d), F32),
        compiler_params=pltpu.CompilerParams(
            dimension_semantics=("arbitrary",), vmem_limit_bytes=VMEM_LIMIT_BYTES),
        name="ple",
    )(x2, p2, g_pre, w_gate, w_proj, g_post)


def kernel(x, p, positions, g_mix_pre, w_in, attn_sinks, hgrn_lb_logits, hgrn_gnorm,
           w_attn_branch, w_hgrn_branch, w_out, g_mix_post, g_ffn_pre, w_gate_up, w_down,
           g_ffn_post, g_ple_pre, w_ple_gate, w_ple_proj, g_ple_post):
    batch, seq, d = x.shape
    t = batch * seq
    depth = w_in.shape[0]
    row = lambda v: v.reshape(1, -1)

    xs = x.reshape(t, d)
    for layer in range(depth):
        proj = _inproj(xs, row(g_mix_pre[layer]), w_in[layer].astype(BF16))
        y_attn, (wa, wh, wo, wgu, wd, wpg) = _attention(
            proj, positions, attn_sinks[layer],
            (w_attn_branch[layer], w_hgrn_branch[layer], w_out[layer], w_gate_up[layer],
             w_down[layer], w_ple_gate[layer]),
            (None, None, None, (FFN_TILE, 2), None, None), batch, seq)
        y_hgrn = _hgrn(proj, hgrn_lb_logits, row(hgrn_gnorm[layer]), layer, batch, seq)
        xs = _merge(xs, y_attn, y_hgrn, proj, wa, wh, wo, row(g_mix_post[layer]))
        xs = _ffn(xs, row(g_ffn_pre[layer]), wgu, wd, row(g_ffn_post[layer]))
        xs = _ple(xs, p[layer].reshape(t, PLE_DIM), row(g_ple_pre[layer]), wpg,
                  w_ple_proj[layer].astype(BF16), row(g_ple_post[layer]))
    return xs.reshape(batch, seq, d)
```

```python
import functools
import math

import numpy as np
import jax
import jax.numpy as jnp
from jax import lax
from jax.experimental import pallas as pl
from jax.experimental.pallas import tpu as pltpu

D_MODEL = 2048
PLE_DIM = 256
ATTN_HEADS = 16
ATTN_KV_HEADS = 4
ATTN_HEAD_DIM = 64
ATTN_Q_WIDTH = ATTN_HEADS * ATTN_HEAD_DIM
ATTN_KV_WIDTH = ATTN_KV_HEADS * ATTN_HEAD_DIM
ATTN_BLOCK = 128
ATTN_BLOCKS_PER_STEP = 4
ATTN_ITEMS_PER_SLOT = 4
ROPE_THETA = 10000.0
HGRN_HEADS = 8
HGRN_DK = 128
HGRN_DV = 128
HGRN_WIDTH = HGRN_HEADS * HGRN_DK
D_FF = 5632
RMS_EPS = 1e-6
IN_SIZES = (ATTN_Q_WIDTH, ATTN_KV_WIDTH, ATTN_KV_WIDTH,
            HGRN_WIDTH, HGRN_WIDTH, HGRN_WIDTH, HGRN_WIDTH, D_MODEL, D_MODEL)
IN_WIDTH = sum(IN_SIZES)
PROJ_OFFSETS = tuple(int(v) for v in np.cumsum((0,) + IN_SIZES[:-1]))

LANES = 128
SUBLANES = 8
HGRN_CHUNK = 128
HGRN_LEVELS = (1, 2, 4, 8, 16, 32, 64)
HGRN_HEADS_PER_STEP = 4
MXU_SUBTILE = 512
FFN_TILE = 512
FFN_DOWN_SUBTILE = 1024
INPROJ_SUBTILE = 1024
VMEM_LIMIT_BYTES = 58 * 1024 * 1024
LOG2E = math.log2(math.e)
MASKED_SCORE = -1e30

BF16 = jnp.bfloat16
F32 = jnp.float32


def _rms_scale(v):
    return lax.rsqrt(jnp.mean(v * v, axis=-1, keepdims=True) + RMS_EPS)


def _sigmoid(v):
    return 0.5 * jnp.tanh(0.5 * v) + 0.5


def _dot(a, b):
    return jnp.dot(a, b, preferred_element_type=F32)


def _block_index(offset, width):
    assert offset % width == 0, (offset, width)
    return offset // width


def _dot_nt(a, b):
    return lax.dot_general(a, b, (((1,), (1,)), ((), ())), preferred_element_type=F32)


def _inproj_kernel(x_ref, g_ref, w_ref, o_ref, h_ref):
    @pl.when(pl.program_id(1) == 0)
    def _():
        x = x_ref[...]
        h_ref[...] = (x * _rms_scale(x) * g_ref[...]).astype(BF16)

    for c0 in range(0, o_ref.shape[1], INPROJ_SUBTILE):
        cols = slice(c0, min(c0 + INPROJ_SUBTILE, o_ref.shape[1]))
        o_ref[:, cols] = _dot(h_ref[...], w_ref[:, cols]).astype(o_ref.dtype)


def _inproj(x2, g, w, tm=1024, tn=2432):
    t, d = x2.shape
    n = w.shape[1]
    assert t % tm == 0 and n % tn == 0
    return pl.pallas_call(
        _inproj_kernel,
        grid=(t // tm, n // tn),
        in_specs=[pl.BlockSpec((tm, d), lambda i, j: (i, 0)),
                  pl.BlockSpec((1, d), lambda i, j: (0, 0)),
                  pl.BlockSpec((d, tn), lambda i, j: (0, j))],
        out_specs=pl.BlockSpec((tm, tn), lambda i, j: (i, j)),
        out_shape=jax.ShapeDtypeStruct((t, n), BF16),
        scratch_shapes=[pltpu.VMEM((tm, d), BF16)],
        compiler_params=pltpu.CompilerParams(
            dimension_semantics=("arbitrary", "arbitrary"),
            vmem_limit_bytes=VMEM_LIMIT_BYTES),
        name="in_proj",
    )(x2, g, w)


def _attn_kernel(sink_ref, pos_ref, invf_ref, ctile_ref, stile_ref, q_ref, k_ref, v_ref, *rest,
                 tile_groups):
    n_w = (len(rest) - 3) // 2
    o_ref, (kz_ref, vz_ref) = rest[n_w], rest[-2:]

    def narrow_weights():
        for w_ref, w16_ref, groups in zip(rest[:n_w], rest[n_w + 1:2 * n_w + 1], tile_groups):
            if groups is None:
                w16_ref[...] = w_ref[...].astype(w16_ref.dtype)
                continue
            tiles = w16_ref.shape[0]
            width = w16_ref.shape[2] // groups
            for tile in range(tiles):
                for g in range(groups):
                    src = (g * tiles + tile) * width
                    w16_ref[tile, :, g * width:(g + 1) * width] = (
                        w_ref[:, src:src + width].astype(w16_ref.dtype))

    blk = ATTN_BLOCK
    nblk = q_ref.shape[0] // blk
    n = pl.program_id(1)
    carry_in = nblk + n % 2
    carry_out = nblk + (n + 1) % 2

    @pl.when(n == 0)
    def _():
        kz_ref[nblk] = jnp.zeros(kz_ref.shape[1:], kz_ref.dtype)
        vz_ref[nblk] = jnp.zeros(vz_ref.shape[1:], vz_ref.dtype)

    ang_t = invf_ref[...] * pos_ref[0]
    q_scale = LOG2E * ATTN_HEAD_DIM ** -0.5

    def spread(tab_t, tile_ref):
        p1 = tab_t.astype(BF16)
        r1 = tab_t - p1.astype(F32)
        p2 = r1.astype(BF16)
        p3 = (r1 - p2.astype(F32)).astype(BF16)
        stacked = jnp.concatenate([p1, p2, p3, jnp.zeros_like(p1)], axis=0)
        return _dot(stacked.astype(F32).T.astype(BF16), tile_ref[...])

    cos_t = jnp.cos(ang_t)
    sin_t = jnp.sin(ang_t)
    cos = spread(cos_t, ctile_ref)
    sin_signed = spread(sin_t, stile_ref)
    cos_q = spread(cos_t * q_scale, ctile_ref)
    sin_q = spread(sin_t * q_scale, stile_ref)

    lane = lax.broadcasted_iota(jnp.int32, (blk, LANES), 1)
    row = lax.broadcasted_iota(jnp.int32, (blk, LANES), 0)
    first_half = (lane & (ATTN_HEAD_DIM // 2)) == 0
    low_head = lane < ATTN_HEAD_DIM
    from_prev = lane > row

    def rope(t, j, c, s):
        rows = slice(j * blk, (j + 1) * blk)
        partner = jnp.where(first_half,
                            pltpu.roll(t, LANES - ATTN_HEAD_DIM // 2, 1),
                            pltpu.roll(t, ATTN_HEAD_DIM // 2, 1))
        return t * c[rows] + partner * s[rows]

    def place(src, dst, j, c):
        slots = (j, carry_out) if j == nblk - 1 else (j,)
        for e in range(2):
            own = jnp.where(low_head if e == 0 else jnp.logical_not(low_head), src, 0.0)
            other = pltpu.roll(own, ATTN_HEAD_DIM, 1)
            lo, hi = (own, other) if e == 0 else (other, own)
            for slot in slots:
                dst[slot, 2 * c + e, 0:blk, 0:LANES] = lo.astype(BF16)
                dst[slot, 2 * c + e, blk:2 * blk, 0:LANES] = hi.astype(BF16)

    def place_keys(j):
        rows = slice(j * blk, (j + 1) * blk)
        for c in range(ATTN_KV_WIDTH // LANES):
            place(rope(k_ref[rows, c * LANES:(c + 1) * LANES].astype(F32), j, cos, sin_signed),
                  kz_ref, j, c)

    def place_values(j):
        rows = slice(j * blk, (j + 1) * blk)
        ones_lo = low_head.astype(BF16)
        for slot in ((j, carry_out) if j == nblk - 1 else (j,)):
            for hk in range(ATTN_KV_HEADS):
                vz_ref[slot, hk, 0:blk, LANES:2 * LANES] = ones_lo
                vz_ref[slot, hk, blk:2 * blk, LANES:2 * LANES] = 1 - ones_lo
        for c in range(ATTN_KV_WIDTH // LANES):
            place(v_ref[rows, c * LANES:(c + 1) * LANES].astype(F32), vz_ref, j, c)

    first_penalty = jnp.where(n > 0, 0.0, MASKED_SCORE)
    items = [(j, hk) for j in range(nblk) for hk in range(ATTN_KV_HEADS)]

    def scores(j, hk):
        rows = slice(j * blk, (j + 1) * blk)
        q = jnp.concatenate(
            [rope(q_ref[rows, g * LANES:(g + 1) * LANES].astype(F32), j, cos_q, sin_q).astype(BF16)
             for g in (2 * hk, 2 * hk + 1)], axis=0)
        prv = carry_in if j == 0 else j - 1
        return _dot_nt(q, kz_ref[prv, hk]), _dot_nt(q, kz_ref[j, hk])

    def fold_and_max(j, hk, s_prev, s_cur):
        folded = []
        for rh in range(2):
            rows = slice(rh * blk, (rh + 1) * blk)
            for ch in range(2):
                sink = sink_ref[4 * hk + 2 * rh + ch] * LOG2E
                cols = slice(ch * blk, (ch + 1) * blk)
                sp = s_prev[rows, cols] + first_penalty if j == 0 else s_prev[rows, cols]
                s = jnp.where(from_prev, sp, s_cur[rows, cols])
                folded.append((s, jnp.maximum(jnp.max(s, axis=-1, keepdims=True), sink), sink))
        return folded

    def weigh(j, hk, folded):
        ps, sink_terms = [], []
        for s, m, sink in folded:
            ps.append(jnp.exp2(s - m))
            sink_terms.append(jnp.exp2(sink - m))
        stack = lambda tiles: jnp.concatenate(
            [jnp.concatenate(tiles[0:2], axis=1), jnp.concatenate(tiles[2:4], axis=1)], axis=0)
        p_prev = stack([jnp.where(from_prev, p, 0.0).astype(BF16) for p in ps])
        p_cur = stack([jnp.where(from_prev, 0.0, p).astype(BF16) for p in ps])
        prv = carry_in if j == 0 else j - 1
        acc = _dot(p_prev, vz_ref[prv, hk]) + _dot(p_cur, vz_ref[j, hk])
        return acc, sink_terms

    def finish(j, hk, acc, sink_terms):
        for rh in range(2):
            rows = slice(rh * blk, (rh + 1) * blk)
            g = 2 * hk + rh
            denom = acc[rows, LANES:] + jnp.where(low_head, sink_terms[2 * rh],
                                                  sink_terms[2 * rh + 1])
            o_ref[j * blk:(j + 1) * blk, g * LANES:(g + 1) * LANES] = (
                acc[rows, :LANES] / denom).astype(o_ref.dtype)

    place_keys(0)
    s_vals, f_vals, w_vals = {}, {}, {}
    width = ATTN_ITEMS_PER_SLOT
    for slot in range(len(items) // width + 3):
        for k in range(slot * width, (slot + 1) * width):
            if k < len(items):
                j, hk = items[k]
                s_vals[k] = scores(j, hk)
                if hk == 0:
                    place_values(j)
                if hk == 1 and j + 1 < nblk:
                    place_keys(j + 1)
        for k in range((slot - 1) * width, slot * width):
            if 0 <= k < len(items):
                f_vals[k] = fold_and_max(*items[k], *s_vals.pop(k))
        for k in range((slot - 2) * width, (slot - 1) * width):
            if 0 <= k < len(items):
                w_vals[k] = weigh(*items[k], f_vals.pop(k))
        for k in range((slot - 3) * width, (slot - 2) * width):
            if 0 <= k < len(items):
                finish(*items[k], *w_vals.pop(k))
        if slot == len(items) // width // 2:
            narrow_weights()


def _rope_constants():
    half = ATTN_HEAD_DIM // 2
    inv_freq = ROPE_THETA ** (-np.arange(half, dtype=np.float32) / half)
    k = np.arange(LANES)[:, None]
    lane = np.arange(LANES)[None, :]
    hit = ((k < 3 * half) & (k % half == lane % half)).astype(np.float32)
    sign = np.where(lane % ATTN_HEAD_DIM < half, -1.0, 1.0).astype(np.float32)
    return inv_freq.reshape(half, 1), hit, hit * sign


def _attention(proj, positions, sinks, weights, tilings, batch, seq):
    blk = ATTN_BLOCK
    rows = ATTN_BLOCKS_PER_STEP * blk
    assert seq % rows == 0
    nb = seq // rows
    t = batch * seq
    steps = batch * nb
    w_specs, w16_specs, w16_shapes = [], [], []
    for w, tiling in zip(weights, tilings):
        assert w.shape[0] % (steps * 2 * SUBLANES) == 0, w.shape
        blk_rows = w.shape[0] // steps
        w_specs.append(pl.BlockSpec((blk_rows, w.shape[1]), lambda b, n: (b * nb + n, 0)))
        if tiling is None:
            w16_specs.append(w_specs[-1])
            w16_shapes.append(jax.ShapeDtypeStruct(w.shape, BF16))
        else:
            width = tiling[0] * tiling[1]
            tiles = w.shape[1] // width
            w16_specs.append(pl.BlockSpec((tiles, blk_rows, width), lambda b, n: (0, b * nb + n, 0)))
            w16_shapes.append(jax.ShapeDtypeStruct((tiles, w.shape[0], width), BF16))
    tile_groups = tuple(None if tiling is None else tiling[1] for tiling in tilings)
    inv_freq, ctile, stile = _rope_constants()
    pos_rows = positions.astype(F32).reshape(t // rows, 1, rows)
    const = lambda a: pl.BlockSpec(a.shape, lambda b, n: (0, 0))
    q_blk, k_blk, v_blk = (_block_index(PROJ_OFFSETS[0], ATTN_Q_WIDTH),
                           _block_index(PROJ_OFFSETS[1], ATTN_KV_WIDTH),
                           _block_index(PROJ_OFFSETS[2], ATTN_KV_WIDTH))
    slots = ATTN_BLOCKS_PER_STEP + 2
    k_scratch = pltpu.VMEM((slots, ATTN_KV_HEADS, 2 * blk, LANES), BF16)
    v_scratch = pltpu.VMEM((slots, ATTN_KV_HEADS, 2 * blk, 2 * LANES), BF16)
    outs = pl.pallas_call(
        functools.partial(_attn_kernel, tile_groups=tile_groups),
        grid=(batch, nb),
        in_specs=[pl.BlockSpec(memory_space=pltpu.SMEM),
                  pl.BlockSpec((1, 1, rows), lambda b, n: (b * nb + n, 0, 0)),
                  const(inv_freq), const(ctile), const(stile),
                  pl.BlockSpec((rows, ATTN_Q_WIDTH), lambda b, n: (b * nb + n, q_blk)),
                  pl.BlockSpec((rows, ATTN_KV_WIDTH), lambda b, n: (b * nb + n, k_blk)),
                  pl.BlockSpec((rows, ATTN_KV_WIDTH), lambda b, n: (b * nb + n, v_blk))] + w_specs,
        out_specs=[pl.BlockSpec((rows, ATTN_Q_WIDTH), lambda b, n: (b * nb + n, 0))] + w16_specs,
        out_shape=[jax.ShapeDtypeStruct((t, ATTN_Q_WIDTH), BF16)] + w16_shapes,
        scratch_shapes=[k_scratch, v_scratch],
        compiler_params=pltpu.CompilerParams(
            dimension_semantics=("arbitrary", "arbitrary"),
            vmem_limit_bytes=VMEM_LIMIT_BYTES),
        name="swa_attention",
    )(sinks, pos_rows, jnp.asarray(inv_freq), jnp.asarray(ctile, dtype=BF16),
      jnp.asarray(stile, dtype=BF16), proj, proj, proj, *weights)
    return outs[0], outs[1:]


def _hgrn_constants():
    c = HGRN_CHUNK
    u = np.arange(c)[None, :]
    t = np.arange(c)[:, None]
    ranges, masks = [], []
    for m in HGRN_LEVELS:
        upper = (t & m) != 0
        start = (t // m) * m
        q_rng = (u >= start) & (u <= t)
        k_rng = (u > t) & (u <= start + m - 1)
        if m < SUBLANES:
            ranges.append(np.where(upper, q_rng, k_rng))
        masks.append(upper & ((u & m) == 0) & ((t // (2 * m)) == (u // (2 * m))))
    ranges.append(u <= t)
    masks.append(u == t)
    rng = np.concatenate(ranges, axis=0).astype(np.float32)
    return np.concatenate([rng, rng], axis=1), np.concatenate(masks, axis=0).astype(np.float32)


def _hgrn_kernel(lbl_ref, gn_ref, w_ref, m_ref, q_ref, f_ref, i_ref, g_ref, o_ref, st_ref,
                 *, layer):
    c = HGRN_CHUNK
    nl = len(HGRN_LEVELS)
    n_fine = sum(m < SUBLANES for m in HGRN_LEVELS)
    heads = q_ref.shape[1] // HGRN_DK
    n_chunks = q_ref.shape[0] // c

    @pl.when(pl.program_id(2) == 0)
    def _():
        st_ref[...] = jnp.zeros_like(st_ref)

    lg = lbl_ref[...]
    eg = jnp.exp(lg - jnp.max(lg, axis=0, keepdims=True))
    lb_all = (jnp.sum(eg[0:layer + 1, :], axis=0, keepdims=True)
              / jnp.sum(eg, axis=0, keepdims=True))
    gn = gn_ref[...]

    head_cols = [slice(h * HGRN_DK, (h + 1) * HGRN_DK) for h in range(heads)]

    def gates(ci):
        rows = slice(ci * c, (ci + 1) * c)
        qts, kks, pieces = [], [], []
        for cols in head_cols:
            f_half = 0.5 * (1.0 - lb_all[:, cols])
            f_mid = lb_all[:, cols] + f_half
            z = f_ref[rows, cols].astype(F32)
            qr = q_ref[rows, cols].astype(F32)
            ft = f_half * jnp.tanh(0.5 * z)
            logf = jnp.log2(f_mid + ft)
            kks.append(f_half - ft)
            qh = (0.5 * HGRN_DK ** -0.5) * qr
            qts.append(qh + qh * jnp.tanh(0.5 * qr))
            hi = logf.astype(BF16)
            pieces.append((hi, (logf - hi.astype(F32)).astype(BF16)))
        split = jnp.concatenate(
            [jnp.concatenate([p[0] for p in pieces], axis=1),
             jnp.concatenate([p[1] for p in pieces], axis=1)], axis=0)
        return qts, kks, _dot(w_ref[...], split)

    def coarse_exponent(b, m):
        pieces = []
        for lo in range(0, c, 2 * m):
            ref = b[lo + m - 1:lo + m, :]
            pieces += [ref - b[lo:lo + m, :], b[lo + m:lo + 2 * m, :] - ref]
        return jnp.concatenate(pieces, axis=0)

    def intra(ci, qts, kks, xs):
        q16 = [qt.astype(BF16) for qt in qts]
        k16 = [kk.astype(BF16) for kk in kks]
        bs = [xs[n_fine * c:(n_fine + 1) * c, cols] for cols in head_cols]
        accs = [m_ref[nl * c:(nl + 1) * c, :] * _dot_nt(q16[h], k16[h]) for h in range(heads)]
        for li, m in enumerate(HGRN_LEVELS):
            for h, cols in enumerate(head_cols):
                x = xs[li * c:(li + 1) * c, cols] if li < n_fine else coarse_exponent(bs[h], m)
                e = jnp.exp2(x).astype(BF16)
                accs[h] = accs[h] + m_ref[li * c:(li + 1) * c, :] * _dot_nt(q16[h] * e, k16[h] * e)
        parts = []
        for h, cols in enumerate(head_cols):
            b = bs[h]
            eb = jnp.exp2(b)
            el = jnp.exp2(b[c - 1:c, :] - b)
            v = i_ref[ci * c:(ci + 1) * c, cols]
            parts.append((q16[h] * eb.astype(BF16), _dot(accs[h].astype(BF16), v), eb[c - 1:c, :],
                          _dot(v.astype(F32).T.astype(BF16), k16[h] * el.astype(BF16))))
        return parts

    def carry_state(ci, parts, states):
        rows = slice(ci * c, (ci + 1) * c)
        new_states = []
        for h, cols in enumerate(head_cols):
            q_decayed, o_intra, chunk_decay, kv = parts[h]
            st = states[h]
            o = _dot_nt(q_decayed, st.astype(BF16)) + o_intra
            new_states.append(st * chunk_decay + kv)
            gate = g_ref[rows, cols].astype(F32)
            y = o * _rms_scale(o) * gn
            gh = 0.5 * gate
            y = y * (gh + gh * jnp.tanh(gh))
            o_ref[rows, cols] = y.astype(o_ref.dtype)
        return new_states

    states = [st_ref[h] for h in range(heads)]
    g_vals, i_vals = {0: gates(0)}, {}
    for k in range(n_chunks + 1):
        if k + 1 < n_chunks:
            g_vals[k + 1] = gates(k + 1)
        if k < n_chunks:
            i_vals[k] = intra(k, *g_vals.pop(k))
        if k >= 1:
            states = carry_state(k - 1, i_vals.pop(k - 1), states)
    for h in range(heads):
        st_ref[h] = states[h]


def _hgrn(proj, lb_logits, gnorm, layer, batch, seq, rows=1024):
    t = batch * seq
    steps = seq // rows
    hp = HGRN_HEADS_PER_STEP
    width = hp * HGRN_DK
    q_blk, f_blk, i_blk, g_blk = (_block_index(PROJ_OFFSETS[s], width) for s in (3, 4, 5, 6))
    rng, masks = _hgrn_constants()
    rng = jnp.asarray(rng, dtype=BF16)
    masks = jnp.asarray(masks, dtype=F32)
    const = lambda a: pl.BlockSpec(a.shape, lambda b, h, s: (0, 0))
    row_spec = lambda base: pl.BlockSpec(
        (rows, width), lambda b, h, s, base=base: (b * steps + s, base + h))
    return pl.pallas_call(
        functools.partial(_hgrn_kernel, layer=layer),
        grid=(batch, HGRN_HEADS // hp, steps),
        in_specs=[pl.BlockSpec((lb_logits.shape[0], width), lambda b, h, s: (0, h)),
                  pl.BlockSpec((1, HGRN_DV), lambda b, h, s: (0, 0)),
                  const(rng), const(masks),
                  row_spec(q_blk), row_spec(f_blk), row_spec(i_blk), row_spec(g_blk)],
        out_specs=pl.BlockSpec((rows, width), lambda b, h, s: (b * steps + s, h)),
        out_shape=jax.ShapeDtypeStruct((t, HGRN_WIDTH), BF16),
        scratch_shapes=[pltpu.VMEM((hp, HGRN_DV, HGRN_DK), F32)],
        compiler_params=pltpu.CompilerParams(
            dimension_semantics=("arbitrary", "arbitrary", "arbitrary")),
        name="hgrn2",
    )(lb_logits, gnorm, rng, masks, proj, proj, proj, proj)


def _merge_kernel(x_ref, ya_ref, yh_ref, wa_ref, wh_ref, wo_ref, gp_ref, *rest):
    n_tiles = (len(rest) - 2) // 2
    ga_refs, gb_refs = rest[:n_tiles], rest[n_tiles:2 * n_tiles]
    o_ref, merged_ref = rest[2 * n_tiles:]
    ya, yh = ya_ref[...], yh_ref[...]
    tile_cols = [slice(c * MXU_SUBTILE, (c + 1) * MXU_SUBTILE) for c in range(n_tiles)]
    for cols, ga_ref, gb_ref in zip(tile_cols, ga_refs, gb_refs):
        merged_ref[:, cols] = (
            _sigmoid(ga_ref[...].astype(F32)) * _dot(ya, wa_ref[:, cols])
            + _sigmoid(gb_ref[...].astype(F32)) * _dot(yh, wh_ref[:, cols])).astype(BF16)
    merged = merged_ref[...]
    for cols in tile_cols:
        o_ref[:, cols] = _dot(merged, wo_ref[:, cols])
    m2 = o_ref[...]
    o_ref[...] = x_ref[...] + m2 * _rms_scale(m2) * gp_ref[...]


def _merge(x2, y_attn, y_hgrn, proj, wa, wh, wo, g_post, tm=512):
    t, d = x2.shape
    n_tiles = d // MXU_SUBTILE
    ga_blk, gb_blk = (_block_index(PROJ_OFFSETS[s], MXU_SUBTILE) for s in (7, 8))
    const = lambda shape: pl.BlockSpec(shape, lambda i: (0, 0), pipeline_mode=pl.Buffered(1))
    gate_specs = [pl.BlockSpec((tm, MXU_SUBTILE), lambda i, blk=base + c: (i, blk))
                  for base in (ga_blk, gb_blk) for c in range(n_tiles)]
    return pl.pallas_call(
        _merge_kernel,
        grid=(t // tm,),
        in_specs=[pl.BlockSpec((tm, d), lambda i: (i, 0)),
                  pl.BlockSpec((tm, ATTN_Q_WIDTH), lambda i: (i, 0)),
                  pl.BlockSpec((tm, HGRN_WIDTH), lambda i: (i, 0)),
                  const(wa.shape), const(wh.shape), const(wo.shape), const((1, d))] + gate_specs,
        out_specs=pl.BlockSpec((tm, d), lambda i: (i, 0)),
        out_shape=jax.ShapeDtypeStruct((t, d), F32),
        scratch_shapes=[pltpu.VMEM((tm, d), BF16)],
        compiler_params=pltpu.CompilerParams(
            dimension_semantics=("arbitrary",), vmem_limit_bytes=VMEM_LIMIT_BYTES),
        name="merge_out",
    )(x2, y_attn, y_hgrn, wa, wh, wo, g_post, *([proj] * (2 * n_tiles)))


def _ffn_kernel(x_ref, gpre_ref, wgu_hbm, wd_hbm, gpost_ref, o_ref, h_ref, wgu_buf, wd_buf, sem):
    i = pl.program_id(0)
    tf = wd_buf.shape[1]
    n_tiles = D_FF // tf
    ring = wgu_buf.shape[0] - 1
    slot_of = lambda j: ring if j == 0 else (j - 1) % ring

    def weight_copies(j):
        slot = slot_of(j)
        return (pltpu.make_async_copy(wgu_hbm.at[j], wgu_buf.at[slot], sem.at[0, slot]),
                pltpu.make_async_copy(wd_hbm.at[pl.ds(j * tf, tf), :], wd_buf.at[slot],
                                      sem.at[1, slot]))

    @pl.when(i == 0)
    def _():
        for copy in weight_copies(0):
            copy.start()

    x = x_ref[...]
    h_ref[...] = (x * _rms_scale(x) * gpre_ref[...]).astype(BF16)
    h = h_ref[...]
    for copy in weight_copies(1):
        copy.start()
    for j in range(n_tiles):
        for copy in weight_copies(j):
            copy.wait()
        slot = slot_of(j)
        gate_up = _dot(h, wgu_buf[slot])
        gate, up = gate_up[:, :tf], gate_up[:, tf:]
        if j + 2 < n_tiles:
            for copy in weight_copies(j + 2):
                copy.start()
        if j == 1:
            for copy in weight_copies(0):
                copy.start()
        act = (gate * _sigmoid(gate) * up).astype(BF16)
        for c0 in range(0, o_ref.shape[1], FFN_DOWN_SUBTILE):
            cols = slice(c0, c0 + FFN_DOWN_SUBTILE)
            down = _dot(act, wd_buf[slot, :, cols])
            o_ref[:, cols] = down if j == 0 else o_ref[:, cols] + down
    y = o_ref[...]
    o_ref[...] = x_ref[...] + y * _rms_scale(y) * gpost_ref[...]

    @pl.when(i == pl.num_programs(0) - 1)
    def _():
        for copy in weight_copies(0):
            copy.wait()


def _ffn(x1, g_pre, w_gate_up, w_down, g_post, tm=512):
    t, d = x1.shape
    tf = FFN_TILE
    assert w_gate_up.shape == (D_FF // tf, d, 2 * tf)
    slots = 4
    return pl.pallas_call(
        _ffn_kernel,
        grid=(t // tm,),
        in_specs=[pl.BlockSpec((tm, d), lambda i: (i, 0)),
                  pl.BlockSpec((1, d), lambda i: (0, 0)),
                  pl.BlockSpec(memory_space=pl.ANY),
                  pl.BlockSpec(memory_space=pl.ANY),
                  pl.BlockSpec((1, d), lambda i: (0, 0))],
        out_specs=pl.BlockSpec((tm, d), lambda i: (i, 0)),
        out_shape=jax.ShapeDtypeStruct((t, d), F32),
        scratch_shapes=[pltpu.VMEM((tm, d), BF16),
                        pltpu.VMEM((slots, d, 2 * tf), BF16), pltpu.VMEM((slots, tf, d), BF16),
                        pltpu.SemaphoreType.DMA((2, slots))],
        compiler_params=pltpu.CompilerParams(
            dimension_semantics=("arbitrary",), vmem_limit_bytes=VMEM_LIMIT_BYTES),
        name="swiglu_ffn",
    )(x1, g_pre, w_gate_up, w_down, g_post)


def _ple_kernel(x_ref, p_ref, gpre_ref, wg_ref, wp_ref, gpost_ref, o_ref):
    x = x_ref[...]
    h = (x * _rms_scale(x) * gpre_ref[...]).astype(BF16)
    p16 = p_ref[...].astype(BF16)
    for c0 in range(0, o_ref.shape[1], MXU_SUBTILE):
        cols = slice(c0, c0 + MXU_SUBTILE)
        gate = _sigmoid(_dot(h, wg_ref[:, cols]))
        o_ref[:, cols] = _dot(p16, wp_ref[:, cols]) * gate
    e = o_ref[...]
    o_ref[...] = x_ref[...] + e * _rms_scale(e) * gpost_ref[...]


def _ple(x2, p2, g_pre, w_gate, w_proj, g_post, tm=512):
    t, d = x2.shape
    const = lambda shape: pl.BlockSpec(shape, lambda i: (0, 0), pipeline_mode=pl.Buffered(1))
    return pl.pallas_call(
        _ple_kernel,
        grid=(t // tm,),
        in_specs=[pl.BlockSpec((tm, d), lambda i: (i, 0)),
                  pl.BlockSpec((tm, PLE_DIM), lambda i: (i, 0)),
                  const((1, d)), const(w_gate.shape), const(w_proj.shape), const((1, d))],
        out_specs=pl.BlockSpec((tm, d), lambda i: (i, 0)),
        out_shape=jax.ShapeDtypeStruct((t, d), F32),
        compiler_params=pltpu.CompilerParams(
            dimension_semantics=("arbitrary",), vmem_limit_bytes=VMEM_LIMIT_BYTES),
        name="ple",
    )(x2, p2, g_pre, w_gate, w_proj, g_post)


def kernel(x, p, positions, g_mix_pre, w_in, attn_sinks, hgrn_lb_logits, hgrn_gnorm,
           w_attn_branch, w_hgrn_branch, w_out, g_mix_post, g_ffn_pre, w_gate_up, w_down,
           g_ffn_post, g_ple_pre, w_ple_gate, w_ple_proj, g_ple_post):
    batch, seq, d = x.shape
    t = batch * seq
    depth = w_in.shape[0]
    row = lambda v: v.reshape(1, -1)

    xs = x.reshape(t, d)
    for layer in range(depth):
        proj = _inproj(xs, row(g_mix_pre[layer]), w_in[layer].astype(BF16))
        y_attn, (wa, wh, wo, wgu, wd, wpg) = _attention(
            proj, positions, attn_sinks[layer],
            (w_attn_branch[layer], w_hgrn_branch[layer], w_out[layer], w_gate_up[layer],
             w_down[layer], w_ple_gate[layer]),
            (None, None, None, (FFN_TILE, 2), None, None), batch, seq)
        y_hgrn = _hgrn(proj, hgrn_lb_logits, row(hgrn_gnorm[layer]), layer, batch, seq)
        xs = _merge(xs, y_attn, y_hgrn, proj, wa, wh, wo, row(g_mix_post[layer]))
        xs = _ffn(xs, row(g_ffn_pre[layer]), wgu, wd, row(g_ffn_post[layer]))
        xs = _ple(xs, p[layer].reshape(t, PLE_DIM), row(g_ple_pre[layer]), wpg,
                  w_ple_proj[layer].astype(BF16), row(g_ple_post[layer]))
    return xs.reshape(batch, seq, d)
```

```python
import functools
import math

import numpy as np
import jax
import jax.numpy as jnp
from jax import lax
from jax.experimental import pallas as pl
from jax.experimental.pallas import tpu as pltpu

D_MODEL = 2048
PLE_DIM = 256
ATTN_HEADS = 16
ATTN_KV_HEADS = 4
ATTN_HEAD_DIM = 64
ATTN_Q_WIDTH = ATTN_HEADS * ATTN_HEAD_DIM
ATTN_KV_WIDTH = ATTN_KV_HEADS * ATTN_HEAD_DIM
ATTN_BLOCK = 128
ATTN_BLOCKS_PER_STEP = 4
ATTN_ITEMS_PER_SLOT = 4
ROPE_THETA = 10000.0
HGRN_HEADS = 8
HGRN_DK = 128
HGRN_DV = 128
HGRN_WIDTH = HGRN_HEADS * HGRN_DK
D_FF = 5632
RMS_EPS = 1e-6
IN_SIZES = (ATTN_Q_WIDTH, ATTN_KV_WIDTH, ATTN_KV_WIDTH,
            HGRN_WIDTH, HGRN_WIDTH, HGRN_WIDTH, HGRN_WIDTH, D_MODEL, D_MODEL)
IN_WIDTH = sum(IN_SIZES)
PROJ_OFFSETS = tuple(int(v) for v in np.cumsum((0,) + IN_SIZES[:-1]))

LANES = 128
SUBLANES = 8
HGRN_CHUNK = 128
HGRN_LEVELS = (1, 2, 4, 8, 16, 32, 64)
HGRN_CHUNKS_PER_GROUP = 2
HGRN_HEADS_PER_STEP = 4
MXU_SUBTILE = 512
FFN_TILE = 512
FFN_DOWN_SUBTILE = 1024
INPROJ_SUBTILE = 1024
VMEM_LIMIT_BYTES = 58 * 1024 * 1024
LOG2E = math.log2(math.e)
MASKED_SCORE = -1e30

BF16 = jnp.bfloat16
F32 = jnp.float32


def _rms_scale(v):
    return lax.rsqrt(jnp.mean(v * v, axis=-1, keepdims=True) + RMS_EPS)


def _sigmoid(v):
    return 0.5 * jnp.tanh(0.5 * v) + 0.5


def _dot(a, b):
    return jnp.dot(a, b, preferred_element_type=F32)


def _block_index(offset, width):
    assert offset % width == 0, (offset, width)
    return offset // width


def _dot_nt(a, b):
    return lax.dot_general(a, b, (((1,), (1,)), ((), ())), preferred_element_type=F32)


def _inproj_kernel(x_ref, g_ref, w_ref, o_ref, h_ref):
    @pl.when(pl.program_id(1) == 0)
    def _():
        x = x_ref[...]
        h_ref[...] = (x * _rms_scale(x) * g_ref[...]).astype(BF16)

    for c0 in range(0, o_ref.shape[1], INPROJ_SUBTILE):
        cols = slice(c0, min(c0 + INPROJ_SUBTILE, o_ref.shape[1]))
        o_ref[:, cols] = _dot(h_ref[...], w_ref[:, cols]).astype(o_ref.dtype)


def _inproj(x2, g, w, tm=1024, tn=2432):
    t, d = x2.shape
    n = w.shape[1]
    assert t % tm == 0 and n % tn == 0
    return pl.pallas_call(
        _inproj_kernel,
        grid=(t // tm, n // tn),
        in_specs=[pl.BlockSpec((tm, d), lambda i, j: (i, 0)),
                  pl.BlockSpec((1, d), lambda i, j: (0, 0)),
                  pl.BlockSpec((d, tn), lambda i, j: (0, j))],
        out_specs=pl.BlockSpec((tm, tn), lambda i, j: (i, j)),
        out_shape=jax.ShapeDtypeStruct((t, n), BF16),
        scratch_shapes=[pltpu.VMEM((tm, d), BF16)],
        compiler_params=pltpu.CompilerParams(
            dimension_semantics=("arbitrary", "arbitrary"),
            vmem_limit_bytes=VMEM_LIMIT_BYTES),
        name="in_proj",
    )(x2, g, w)


def _attn_kernel(sink_ref, pos_ref, invf_ref, ctile_ref, stile_ref, q_ref, k_ref, v_ref, *rest,
                 tile_groups):
    n_w = (len(rest) - 3) // 2
    o_ref, (kz_ref, vz_ref) = rest[n_w], rest[-2:]

    def narrow_weights():
        for w_ref, w16_ref, groups in zip(rest[:n_w], rest[n_w + 1:2 * n_w + 1], tile_groups):
            if groups is None:
                w16_ref[...] = w_ref[...].astype(w16_ref.dtype)
                continue
            tiles = w16_ref.shape[0]
            width = w16_ref.shape[2] // groups
            for tile in range(tiles):
                for g in range(groups):
                    src = (g * tiles + tile) * width
                    w16_ref[tile, :, g * width:(g + 1) * width] = (
                        w_ref[:, src:src + width].astype(w16_ref.dtype))

    blk = ATTN_BLOCK
    nblk = q_ref.shape[0] // blk
    n = pl.program_id(1)
    carry_in = nblk + n % 2
    carry_out = nblk + (n + 1) % 2

    @pl.when(n == 0)
    def _():
        kz_ref[nblk] = jnp.zeros(kz_ref.shape[1:], kz_ref.dtype)
        vz_ref[nblk] = jnp.zeros(vz_ref.shape[1:], vz_ref.dtype)

    ang_t = invf_ref[...] * pos_ref[0]
    q_scale = LOG2E * ATTN_HEAD_DIM ** -0.5

    def spread(tab_t, tile_ref):
        p1 = tab_t.astype(BF16)
        r1 = tab_t - p1.astype(F32)
        p2 = r1.astype(BF16)
        p3 = (r1 - p2.astype(F32)).astype(BF16)
        stacked = jnp.concatenate([p1, p2, p3, jnp.zeros_like(p1)], axis=0)
        return _dot(stacked.astype(F32).T.astype(BF16), tile_ref[...])

    cos_t = jnp.cos(ang_t)
    sin_t = jnp.sin(ang_t)
    cos = spread(cos_t, ctile_ref)
    sin_signed = spread(sin_t, stile_ref)
    cos_q = spread(cos_t * q_scale, ctile_ref)
    sin_q = spread(sin_t * q_scale, stile_ref)

    lane = lax.broadcasted_iota(jnp.int32, (blk, LANES), 1)
    row = lax.broadcasted_iota(jnp.int32, (blk, LANES), 0)
    first_half = (lane & (ATTN_HEAD_DIM // 2)) == 0
    low_head = lane < ATTN_HEAD_DIM
    from_prev = lane > row

    def rope(t, j, c, s):
        rows = slice(j * blk, (j + 1) * blk)
        partner = jnp.where(first_half,
                            pltpu.roll(t, LANES - ATTN_HEAD_DIM // 2, 1),
                            pltpu.roll(t, ATTN_HEAD_DIM // 2, 1))
        return t * c[rows] + partner * s[rows]

    def place(src, dst, j, c):
        slots = (j, carry_out) if j == nblk - 1 else (j,)
        for e in range(2):
            own = jnp.where(low_head if e == 0 else jnp.logical_not(low_head), src, 0.0)
            other = pltpu.roll(own, ATTN_HEAD_DIM, 1)
            lo, hi = (own, other) if e == 0 else (other, own)
            for slot in slots:
                dst[slot, 2 * c + e, 0:blk, 0:LANES] = lo.astype(BF16)
                dst[slot, 2 * c + e, blk:2 * blk, 0:LANES] = hi.astype(BF16)

    def place_keys(j):
        rows = slice(j * blk, (j + 1) * blk)
        for c in range(ATTN_KV_WIDTH // LANES):
            place(rope(k_ref[rows, c * LANES:(c + 1) * LANES].astype(F32), j, cos, sin_signed),
                  kz_ref, j, c)

    def place_values(j):
        rows = slice(j * blk, (j + 1) * blk)
        ones_lo = low_head.astype(BF16)
        for slot in ((j, carry_out) if j == nblk - 1 else (j,)):
            for hk in range(ATTN_KV_HEADS):
                vz_ref[slot, hk, 0:blk, LANES:2 * LANES] = ones_lo
                vz_ref[slot, hk, blk:2 * blk, LANES:2 * LANES] = 1 - ones_lo
        for c in range(ATTN_KV_WIDTH // LANES):
            place(v_ref[rows, c * LANES:(c + 1) * LANES].astype(F32), vz_ref, j, c)

    first_penalty = jnp.where(n > 0, 0.0, MASKED_SCORE)
    items = [(j, hk) for j in range(nblk) for hk in range(ATTN_KV_HEADS)]

    def scores(j, hk):
        rows = slice(j * blk, (j + 1) * blk)
        q = jnp.concatenate(
            [rope(q_ref[rows, g * LANES:(g + 1) * LANES].astype(F32), j, cos_q, sin_q).astype(BF16)
             for g in (2 * hk, 2 * hk + 1)], axis=0)
        prv = carry_in if j == 0 else j - 1
        return _dot_nt(q, kz_ref[prv, hk]), _dot_nt(q, kz_ref[j, hk])

    def fold_and_max(j, hk, s_prev, s_cur):
        folded = []
        for rh in range(2):
            rows = slice(rh * blk, (rh + 1) * blk)
            for ch in range(2):
                sink = sink_ref[4 * hk + 2 * rh + ch] * LOG2E
                cols = slice(ch * blk, (ch + 1) * blk)
                sp = s_prev[rows, cols] + first_penalty if j == 0 else s_prev[rows, cols]
                s = jnp.where(from_prev, sp, s_cur[rows, cols])
                folded.append((s, jnp.maximum(jnp.max(s, axis=-1, keepdims=True), sink), sink))
        return folded

    def weigh(j, hk, folded):
        ps, sink_terms = [], []
        for s, m, sink in folded:
            ps.append(jnp.exp2(s - m))
            sink_terms.append(jnp.exp2(sink - m))
        stack = lambda tiles: jnp.concatenate(
            [jnp.concatenate(tiles[0:2], axis=1), jnp.concatenate(tiles[2:4], axis=1)], axis=0)
        p_prev = stack([jnp.where(from_prev, p, 0.0).astype(BF16) for p in ps])
        p_cur = stack([jnp.where(from_prev, 0.0, p).astype(BF16) for p in ps])
        prv = carry_in if j == 0 else j - 1
        acc = _dot(p_prev, vz_ref[prv, hk]) + _dot(p_cur, vz_ref[j, hk])
        return acc, sink_terms

    def finish(j, hk, acc, sink_terms):
        for rh in range(2):
            rows = slice(rh * blk, (rh + 1) * blk)
            g = 2 * hk + rh
            denom = acc[rows, LANES:] + jnp.where(low_head, sink_terms[2 * rh],
                                                  sink_terms[2 * rh + 1])
            o_ref[j * blk:(j + 1) * blk, g * LANES:(g + 1) * LANES] = (
                acc[rows, :LANES] / denom).astype(o_ref.dtype)

    place_keys(0)
    s_vals, f_vals, w_vals = {}, {}, {}
    width = ATTN_ITEMS_PER_SLOT
    for slot in range(len(items) // width + 3):
        for k in range(slot * width, (slot + 1) * width):
            if k < len(items):
                j, hk = items[k]
                s_vals[k] = scores(j, hk)
                if hk == 0:
                    place_values(j)
                if hk == 1 and j + 1 < nblk:
                    place_keys(j + 1)
        for k in range((slot - 1) * width, slot * width):
            if 0 <= k < len(items):
                f_vals[k] = fold_and_max(*items[k], *s_vals.pop(k))
        for k in range((slot - 2) * width, (slot - 1) * width):
            if 0 <= k < len(items):
                w_vals[k] = weigh(*items[k], f_vals.pop(k))
        for k in range((slot - 3) * width, (slot - 2) * width):
            if 0 <= k < len(items):
                finish(*items[k], *w_vals.pop(k))
        if slot == len(items) // width // 2:
            narrow_weights()


def _rope_constants():
    half = ATTN_HEAD_DIM // 2
    inv_freq = ROPE_THETA ** (-np.arange(half, dtype=np.float32) / half)
    k = np.arange(LANES)[:, None]
    lane = np.arange(LANES)[None, :]
    hit = ((k < 3 * half) & (k % half == lane % half)).astype(np.float32)
    sign = np.where(lane % ATTN_HEAD_DIM < half, -1.0, 1.0).astype(np.float32)
    return inv_freq.reshape(half, 1), hit, hit * sign


def _attention(proj, positions, sinks, weights, tilings, batch, seq):
    blk = ATTN_BLOCK
    rows = ATTN_BLOCKS_PER_STEP * blk
    assert seq % rows == 0
    nb = seq // rows
    t = batch * seq
    steps = batch * nb
    w_specs, w16_specs, w16_shapes = [], [], []
    for w, tiling in zip(weights, tilings):
        assert w.shape[0] % (steps * 2 * SUBLANES) == 0, w.shape
        blk_rows = w.shape[0] // steps
        w_specs.append(pl.BlockSpec((blk_rows, w.shape[1]), lambda b, n: (b * nb + n, 0)))
        if tiling is None:
            w16_specs.append(w_specs[-1])
            w16_shapes.append(jax.ShapeDtypeStruct(w.shape, BF16))
        else:
            width = tiling[0] * tiling[1]
            tiles = w.shape[1] // width
            w16_specs.append(pl.BlockSpec((tiles, blk_rows, width), lambda b, n: (0, b * nb + n, 0)))
            w16_shapes.append(jax.ShapeDtypeStruct((tiles, w.shape[0], width), BF16))
    tile_groups = tuple(None if tiling is None else tiling[1] for tiling in tilings)
    inv_freq, ctile, stile = _rope_constants()
    pos_rows = positions.astype(F32).reshape(t // rows, 1, rows)
    const = lambda a: pl.BlockSpec(a.shape, lambda b, n: (0, 0))
    q_blk, k_blk, v_blk = (_block_index(PROJ_OFFSETS[0], ATTN_Q_WIDTH),
                           _block_index(PROJ_OFFSETS[1], ATTN_KV_WIDTH),
                           _block_index(PROJ_OFFSETS[2], ATTN_KV_WIDTH))
    slots = ATTN_BLOCKS_PER_STEP + 2
    k_scratch = pltpu.VMEM((slots, ATTN_KV_HEADS, 2 * blk, LANES), BF16)
    v_scratch = pltpu.VMEM((slots, ATTN_KV_HEADS, 2 * blk, 2 * LANES), BF16)
    outs = pl.pallas_call(
        functools.partial(_attn_kernel, tile_groups=tile_groups),
        grid=(batch, nb),
        in_specs=[pl.BlockSpec(memory_space=pltpu.SMEM),
                  pl.BlockSpec((1, 1, rows), lambda b, n: (b * nb + n, 0, 0)),
                  const(inv_freq), const(ctile), const(stile),
                  pl.BlockSpec((rows, ATTN_Q_WIDTH), lambda b, n: (b * nb + n, q_blk)),
                  pl.BlockSpec((rows, ATTN_KV_WIDTH), lambda b, n: (b * nb + n, k_blk)),
                  pl.BlockSpec((rows, ATTN_KV_WIDTH), lambda b, n: (b * nb + n, v_blk))] + w_specs,
        out_specs=[pl.BlockSpec((rows, ATTN_Q_WIDTH), lambda b, n: (b * nb + n, 0))] + w16_specs,
        out_shape=[jax.ShapeDtypeStruct((t, ATTN_Q_WIDTH), BF16)] + w16_shapes,
        scratch_shapes=[k_scratch, v_scratch],
        compiler_params=pltpu.CompilerParams(
            dimension_semantics=("arbitrary", "arbitrary"),
            vmem_limit_bytes=VMEM_LIMIT_BYTES),
        name="swa_attention",
    )(sinks, pos_rows, jnp.asarray(inv_freq), jnp.asarray(ctile, dtype=BF16),
      jnp.asarray(stile, dtype=BF16), proj, proj, proj, *weights)
    return outs[0], outs[1:]


def _hgrn_constants():
    c = HGRN_CHUNK
    u = np.arange(c)[None, :]
    t = np.arange(c)[:, None]
    ranges, masks = [], []
    for m in HGRN_LEVELS:
        upper = (t & m) != 0
        start = (t // m) * m
        q_rng = (u >= start) & (u <= t)
        k_rng = (u > t) & (u <= start + m - 1)
        if m < SUBLANES:
            ranges.append(np.where(upper, q_rng, k_rng))
        masks.append(upper & ((u & m) == 0) & ((t // (2 * m)) == (u // (2 * m))))
    ranges.append(u <= t)
    masks.append(u == t)
    rng = np.concatenate(ranges, axis=0).astype(np.float32)
    return np.concatenate([rng, rng], axis=1), np.concatenate(masks, axis=0).astype(np.float32)


def _hgrn_kernel(lbl_ref, gn_ref, w_ref, m_ref, q_ref, f_ref, i_ref, g_ref, o_ref, st_ref,
                 *, layer):
    c = HGRN_CHUNK
    nl = len(HGRN_LEVELS)
    n_fine = sum(m < SUBLANES for m in HGRN_LEVELS)
    heads = q_ref.shape[1] // HGRN_DK
    n_chunks = q_ref.shape[0] // c

    @pl.when(pl.program_id(2) == 0)
    def _():
        st_ref[...] = jnp.zeros_like(st_ref)

    lg = lbl_ref[...]
    eg = jnp.exp(lg - jnp.max(lg, axis=0, keepdims=True))
    lb_all = (jnp.sum(eg[0:layer + 1, :], axis=0, keepdims=True)
              / jnp.sum(eg, axis=0, keepdims=True))
    gn = gn_ref[...]

    head_cols = [slice(h * HGRN_DK, (h + 1) * HGRN_DK) for h in range(heads)]

    def gates(ci):
        rows = slice(ci * c, (ci + 1) * c)
        qts, kks, pieces = [], [], []
        for cols in head_cols:
            f_half = 0.5 * (1.0 - lb_all[:, cols])
            f_mid = lb_all[:, cols] + f_half
            z = f_ref[rows, cols].astype(F32)
            qr = q_ref[rows, cols].astype(F32)
            ft = f_half * jnp.tanh(0.5 * z)
            logf = jnp.log2(f_mid + ft)
            kks.append(f_half - ft)
            qh = (0.5 * HGRN_DK ** -0.5) * qr
            qts.append(qh + qh * jnp.tanh(0.5 * qr))
            hi = logf.astype(BF16)
            pieces.append((hi, (logf - hi.astype(F32)).astype(BF16)))
        split = jnp.concatenate(
            [jnp.concatenate([p[0] for p in pieces], axis=1),
             jnp.concatenate([p[1] for p in pieces], axis=1)], axis=0)
        return qts, kks, _dot(w_ref[...], split)

    def coarse_exponent(b, m):
        pieces = []
        for lo in range(0, c, 2 * m):
            ref = b[lo + m - 1:lo + m, :]
            pieces += [ref - b[lo:lo + m, :], b[lo + m:lo + 2 * m, :] - ref]
        return jnp.concatenate(pieces, axis=0)

    def intra(group):
        chains = [(ci, cols, qts[h].astype(BF16), kks[h].astype(BF16), xs,
                   xs[n_fine * c:(n_fine + 1) * c, cols])
                  for ci, qts, kks, xs in group for h, cols in enumerate(head_cols)]
        accs = [m_ref[nl * c:(nl + 1) * c, :] * _dot_nt(q16, k16)
                for _, _, q16, k16, _, _ in chains]
        for li, m in enumerate(HGRN_LEVELS):
            mask = m_ref[li * c:(li + 1) * c, :]
            for n, (_, cols, q16, k16, xs, b) in enumerate(chains):
                x = xs[li * c:(li + 1) * c, cols] if li < n_fine else coarse_exponent(b, m)
                e = jnp.exp2(x).astype(BF16)
                accs[n] = accs[n] + mask * _dot_nt(q16 * e, k16 * e)
        parts = {ci: [] for ci, _, _, _ in group}
        for (ci, cols, q16, k16, _, b), acc in zip(chains, accs):
            eb = jnp.exp2(b)
            el = jnp.exp2(b[c - 1:c, :] - b)
            v = i_ref[ci * c:(ci + 1) * c, cols]
            parts[ci].append((q16 * eb.astype(BF16), _dot(acc.astype(BF16), v), eb[c - 1:c, :],
                              _dot(v.astype(F32).T.astype(BF16), k16 * el.astype(BF16))))
        return parts

    def carry_state(ci, parts, states):
        rows = slice(ci * c, (ci + 1) * c)
        new_states = []
        for h, cols in enumerate(head_cols):
            q_decayed, o_intra, chunk_decay, kv = parts[h]
            st = states[h]
            o = _dot_nt(q_decayed, st.astype(BF16)) + o_intra
            new_states.append(st * chunk_decay + kv)
            gate = g_ref[rows, cols].astype(F32)
            y = o * _rms_scale(o) * gn
            gh = 0.5 * gate
            y = y * (gh + gh * jnp.tanh(gh))
            o_ref[rows, cols] = y.astype(o_ref.dtype)
        return new_states

    states = [st_ref[h] for h in range(heads)]
    gs = HGRN_CHUNKS_PER_GROUP
    groups = [list(range(g0, min(g0 + gs, n_chunks))) for g0 in range(0, n_chunks, gs)]
    g_vals = {0: [(ci,) + gates(ci) for ci in groups[0]]}
    i_vals = {}
    for k in range(len(groups) + 1):
        if k >= 1:
            parts = i_vals.pop(k - 1)
            for ci in groups[k - 1]:
                states = carry_state(ci, parts[ci], states)
        if k + 1 < len(groups):
            g_vals[k + 1] = [(ci,) + gates(ci) for ci in groups[k + 1]]
        if k < len(groups):
            i_vals[k] = intra(g_vals.pop(k))
    for h in range(heads):
        st_ref[h] = states[h]


def _hgrn(proj, lb_logits, gnorm, layer, batch, seq, rows=1024):
    t = batch * seq
    steps = seq // rows
    hp = HGRN_HEADS_PER_STEP
    width = hp * HGRN_DK
    q_blk, f_blk, i_blk, g_blk = (_block_index(PROJ_OFFSETS[s], width) for s in (3, 4, 5, 6))
    rng, masks = _hgrn_constants()
    rng = jnp.asarray(rng, dtype=BF16)
    masks = jnp.asarray(masks, dtype=F32)
    const = lambda a: pl.BlockSpec(a.shape, lambda b, h, s: (0, 0))
    row_spec = lambda base: pl.BlockSpec(
        (rows, width), lambda b, h, s, base=base: (b * steps + s, base + h))
    return pl.pallas_call(
        functools.partial(_hgrn_kernel, layer=layer),
        grid=(batch, HGRN_HEADS // hp, steps),
        in_specs=[pl.BlockSpec((lb_logits.shape[0], width), lambda b, h, s: (0, h)),
                  pl.BlockSpec((1, HGRN_DV), lambda b, h, s: (0, 0)),
                  const(rng), const(masks),
                  row_spec(q_blk), row_spec(f_blk), row_spec(i_blk), row_spec(g_blk)],
        out_specs=pl.BlockSpec((rows, width), lambda b, h, s: (b * steps + s, h)),
        out_shape=jax.ShapeDtypeStruct((t, HGRN_WIDTH), BF16),
        scratch_shapes=[pltpu.VMEM((hp, HGRN_DV, HGRN_DK), F32)],
        compiler_params=pltpu.CompilerParams(
            dimension_semantics=("arbitrary", "arbitrary", "arbitrary")),
        name="hgrn2",
    )(lb_logits, gnorm, rng, masks, proj, proj, proj, proj)


def _merge_kernel(x_ref, ya_ref, yh_ref, wa_ref, wh_ref, wo_ref, gp_ref, *rest):
    n_tiles = (len(rest) - 2) // 2
    ga_refs, gb_refs = rest[:n_tiles], rest[n_tiles:2 * n_tiles]
    o_ref, merged_ref = rest[2 * n_tiles:]
    ya, yh = ya_ref[...], yh_ref[...]
    tile_cols = [slice(c * MXU_SUBTILE, (c + 1) * MXU_SUBTILE) for c in range(n_tiles)]
    for cols, ga_ref, gb_ref in zip(tile_cols, ga_refs, gb_refs):
        merged_ref[:, cols] = (
            _sigmoid(ga_ref[...].astype(F32)) * _dot(ya, wa_ref[:, cols])
            + _sigmoid(gb_ref[...].astype(F32)) * _dot(yh, wh_ref[:, cols])).astype(BF16)
    merged = merged_ref[...]
    for cols in tile_cols:
        o_ref[:, cols] = _dot(merged, wo_ref[:, cols])
    m2 = o_ref[...]
    o_ref[...] = x_ref[...] + m2 * _rms_scale(m2) * gp_ref[...]


def _merge(x2, y_attn, y_hgrn, proj, wa, wh, wo, g_post, tm=512):
    t, d = x2.shape
    n_tiles = d // MXU_SUBTILE
    ga_blk, gb_blk = (_block_index(PROJ_OFFSETS[s], MXU_SUBTILE) for s in (7, 8))
    const = lambda shape: pl.BlockSpec(shape, lambda i: (0, 0), pipeline_mode=pl.Buffered(1))
    gate_specs = [pl.BlockSpec((tm, MXU_SUBTILE), lambda i, blk=base + c: (i, blk))
                  for base in (ga_blk, gb_blk) for c in range(n_tiles)]
    return pl.pallas_call(
        _merge_kernel,
        grid=(t // tm,),
        in_specs=[pl.BlockSpec((tm, d), lambda i: (i, 0)),
                  pl.BlockSpec((tm, ATTN_Q_WIDTH), lambda i: (i, 0)),
                  pl.BlockSpec((tm, HGRN_WIDTH), lambda i: (i, 0)),
                  const(wa.shape), const(wh.shape), const(wo.shape), const((1, d))] + gate_specs,
        out_specs=pl.BlockSpec((tm, d), lambda i: (i, 0)),
        out_shape=jax.ShapeDtypeStruct((t, d), F32),
        scratch_shapes=[pltpu.VMEM((tm, d), BF16)],
        compiler_params=pltpu.CompilerParams(
            dimension_semantics=("arbitrary",), vmem_limit_bytes=VMEM_LIMIT_BYTES),
        name="merge_out",
    )(x2, y_attn, y_hgrn, wa, wh, wo, g_post, *([proj] * (2 * n_tiles)))


def _ffn_kernel(x_ref, gpre_ref, wgu_hbm, wd_hbm, gpost_ref, o_ref, h_ref, wgu_buf, wd_buf, sem):
    i = pl.program_id(0)
    tf = wd_buf.shape[1]
    n_tiles = D_FF // tf
    ring = wgu_buf.shape[0] - 1
    slot_of = lambda j: ring if j == 0 else (j - 1) % ring

    def weight_copies(j):
        slot = slot_of(j)
        return (pltpu.make_async_copy(wgu_hbm.at[j], wgu_buf.at[slot], sem.at[0, slot]),
                pltpu.make_async_copy(wd_hbm.at[pl.ds(j * tf, tf), :], wd_buf.at[slot],
                                      sem.at[1, slot]))

    @pl.when(i == 0)
    def _():
        for copy in weight_copies(0):
            copy.start()

    x = x_ref[...]
    h_ref[...] = (x * _rms_scale(x) * gpre_ref[...]).astype(BF16)
    h = h_ref[...]
    ahead = ring - 1
    for t in range(1, ahead):
        for copy in weight_copies(t):
            copy.start()
    for j in range(n_tiles):
        for copy in weight_copies(j):
            copy.wait()
        slot = slot_of(j)
        gate_up = _dot(h, wgu_buf[slot])
        gate, up = gate_up[:, :tf], gate_up[:, tf:]
        if j + ahead < n_tiles:
            for copy in weight_copies(j + ahead):
                copy.start()
        if j == 1:
            for copy in weight_copies(0):
                copy.start()
        act = (gate * _sigmoid(gate) * up).astype(BF16)
        for c0 in range(0, o_ref.shape[1], FFN_DOWN_SUBTILE):
            cols = slice(c0, c0 + FFN_DOWN_SUBTILE)
            down = _dot(act, wd_buf[slot, :, cols])
            o_ref[:, cols] = down if j == 0 else o_ref[:, cols] + down
    y = o_ref[...]
    o_ref[...] = x_ref[...] + y * _rms_scale(y) * gpost_ref[...]

    @pl.when(i == pl.num_programs(0) - 1)
    def _():
        for copy in weight_copies(0):
            copy.wait()


def _ffn(x1, g_pre, w_gate_up, w_down, g_post, tm=512):
    t, d = x1.shape
    tf = FFN_TILE
    assert w_gate_up.shape == (D_FF // tf, d, 2 * tf)
    slots = 5
    return pl.pallas_call(
        _ffn_kernel,
        grid=(t // tm,),
        in_specs=[pl.BlockSpec((tm, d), lambda i: (i, 0)),
                  pl.BlockSpec((1, d), lambda i: (0, 0)),
                  pl.BlockSpec(memory_space=pl.ANY),
                  pl.BlockSpec(memory_space=pl.ANY),
                  pl.BlockSpec((1, d), lambda i: (0, 0))],
        out_specs=pl.BlockSpec((tm, d), lambda i: (i, 0)),
        out_shape=jax.ShapeDtypeStruct((t, d), F32),
        scratch_shapes=[pltpu.VMEM((tm, d), BF16),
                        pltpu.VMEM((slots, d, 2 * tf), BF16), pltpu.VMEM((slots, tf, d), BF16),
                        pltpu.SemaphoreType.DMA((2, slots))],
        compiler_params=pltpu.CompilerParams(
            dimension_semantics=("arbitrary",), vmem_limit_bytes=VMEM_LIMIT_BYTES),
        name="swiglu_ffn",
    )(x1, g_pre, w_gate_up, w_down, g_post)


def _ple_kernel(x_ref, p_ref, gpre_ref, wg_ref, wp_ref, gpost_ref, o_ref):
    x = x_ref[...]
    h = (x * _rms_scale(x) * gpre_ref[...]).astype(BF16)
    p16 = p_ref[...].astype(BF16)
    for c0 in range(0, o_ref.shape[1], MXU_SUBTILE):
        cols = slice(c0, c0 + MXU_SUBTILE)
        gate = _sigmoid(_dot(h, wg_ref[:, cols]))
        o_ref[:, cols] = _dot(p16, wp_ref[:, cols]) * gate
    e = o_ref[...]
    o_ref[...] = x_ref[...] + e * _rms_scale(e) * gpost_ref[...]


def _ple(x2, p2, g_pre, w_gate, w_proj, g_post, tm=512):
    t, d = x2.shape
    const = lambda shape: pl.BlockSpec(shape, lambda i: (0, 0), pipeline_mode=pl.Buffered(1))
    return pl.pallas_call(
        _ple_kernel,
        grid=(t // tm,),
        in_specs=[pl.BlockSpec((tm, d), lambda i: (i, 0)),
                  pl.BlockSpec((tm, PLE_DIM), lambda i: (i, 0)),
                  const((1, d)), const(w_gate.shape), const(w_proj.shape), const((1, d))],
        out_specs=pl.BlockSpec((tm, d), lambda i: (i, 0)),
        out_shape=jax.ShapeDtypeStruct((t, d), F32),
        compiler_params=pltpu.CompilerParams(
            dimension_semantics=("arbitrary",), vmem_limit_bytes=VMEM_LIMIT_BYTES),
        name="ple",
    )(x2, p2, g_pre, w_gate, w_proj, g_post)


def kernel(x, p, positions, g_mix_pre, w_in, attn_sinks, hgrn_lb_logits, hgrn_gnorm,
           w_attn_branch, w_hgrn_branch, w_out, g_mix_post, g_ffn_pre, w_gate_up, w_down,
           g_ffn_post, g_ple_pre, w_ple_gate, w_ple_proj, g_ple_post):
    batch, seq, d = x.shape
    t = batch * seq
    depth = w_in.shape[0]
    row = lambda v: v.reshape(1, -1)

    xs = x.reshape(t, d)
    for layer in range(depth):
        proj = _inproj(xs, row(g_mix_pre[layer]), w_in[layer].astype(BF16))
        y_attn, (wa, wh, wo, wgu, wd, wpg) = _attention(
            proj, positions, attn_sinks[layer],
            (w_attn_branch[layer], w_hgrn_branch[layer], w_out[layer], w_gate_up[layer],
             w_down[layer], w_ple_gate[layer]),
            (None, None, None, (FFN_TILE, 2), None, None), batch, seq)
        y_hgrn = _hgrn(proj, hgrn_lb_logits, row(hgrn_gnorm[layer]), layer, batch, seq)
        xs = _merge(xs, y_attn, y_hgrn, proj, wa, wh, wo, row(g_mix_post[layer]))
        xs = _ffn(xs, row(g_ffn_pre[layer]), wgu, wd, row(g_ffn_post[layer]))
        xs = _ple(xs, p[layer].reshape(t, PLE_DIM), row(g_ple_pre[layer]), wpg,
                  w_ple_proj[layer].astype(BF16), row(g_ple_post[layer]))
    return xs.reshape(batch, seq, d)
```

```python
import functools
import math

import numpy as np
import jax
import jax.numpy as jnp
from jax import lax
from jax.experimental import pallas as pl
from jax.experimental.pallas import tpu as pltpu

D_MODEL = 2048
PLE_DIM = 256
ATTN_HEADS = 16
ATTN_KV_HEADS = 4
ATTN_HEAD_DIM = 64
ATTN_Q_WIDTH = ATTN_HEADS * ATTN_HEAD_DIM
ATTN_KV_WIDTH = ATTN_KV_HEADS * ATTN_HEAD_DIM
ATTN_BLOCK = 128
ATTN_BLOCKS_PER_STEP = 4
ATTN_ITEMS_PER_SLOT = 4
ROPE_THETA = 10000.0
HGRN_HEADS = 8
HGRN_DK = 128
HGRN_DV = 128
HGRN_WIDTH = HGRN_HEADS * HGRN_DK
D_FF = 5632
RMS_EPS = 1e-6
IN_SIZES = (ATTN_Q_WIDTH, ATTN_KV_WIDTH, ATTN_KV_WIDTH,
            HGRN_WIDTH, HGRN_WIDTH, HGRN_WIDTH, HGRN_WIDTH, D_MODEL, D_MODEL)
IN_WIDTH = sum(IN_SIZES)
PROJ_OFFSETS = tuple(int(v) for v in np.cumsum((0,) + IN_SIZES[:-1]))

LANES = 128
SUBLANES = 8
HGRN_CHUNK = 128
HGRN_LEVELS = (1, 2, 4, 8, 16, 32, 64)
HGRN_CHUNKS_PER_GROUP = 2
HGRN_HEADS_PER_STEP = 4
MXU_SUBTILE = 512
FFN_TILE = 512
FFN_DOWN_SUBTILE = 1024
INPROJ_SUBTILE = 1024
VMEM_LIMIT_BYTES = 58 * 1024 * 1024
LOG2E = math.log2(math.e)
MASKED_SCORE = -1e30

BF16 = jnp.bfloat16
F32 = jnp.float32


def _rms_scale(v):
    return lax.rsqrt(jnp.mean(v * v, axis=-1, keepdims=True) + RMS_EPS)


def _sigmoid(v):
    return 0.5 * jnp.tanh(0.5 * v) + 0.5


def _dot(a, b):
    return jnp.dot(a, b, preferred_element_type=F32)


def _block_index(offset, width):
    assert offset % width == 0, (offset, width)
    return offset // width


def _dot_nt(a, b):
    return lax.dot_general(a, b, (((1,), (1,)), ((), ())), preferred_element_type=F32)


def _inproj_kernel(x_ref, g_ref, w_ref, o_ref, h_ref):
    @pl.when(pl.program_id(1) == 0)
    def _():
        x = x_ref[...]
        h_ref[...] = (x * _rms_scale(x) * g_ref[...]).astype(BF16)

    for c0 in range(0, o_ref.shape[1], INPROJ_SUBTILE):
        cols = slice(c0, min(c0 + INPROJ_SUBTILE, o_ref.shape[1]))
        o_ref[:, cols] = _dot(h_ref[...], w_ref[:, cols]).astype(o_ref.dtype)


def _inproj(x2, g, w, tm=1024, tn=2432):
    t, d = x2.shape
    n = w.shape[1]
    assert t % tm == 0 and n % tn == 0
    return pl.pallas_call(
        _inproj_kernel,
        grid=(t // tm, n // tn),
        in_specs=[pl.BlockSpec((tm, d), lambda i, j: (i, 0)),
                  pl.BlockSpec((1, d), lambda i, j: (0, 0)),
                  pl.BlockSpec((d, tn), lambda i, j: (0, j))],
        out_specs=pl.BlockSpec((tm, tn), lambda i, j: (i, j)),
        out_shape=jax.ShapeDtypeStruct((t, n), BF16),
        scratch_shapes=[pltpu.VMEM((tm, d), BF16)],
        compiler_params=pltpu.CompilerParams(
            dimension_semantics=("arbitrary", "arbitrary"),
            vmem_limit_bytes=VMEM_LIMIT_BYTES),
        name="in_proj",
    )(x2, g, w)


def _attn_kernel(sink_ref, pos_ref, invf_ref, ctile_ref, stile_ref, q_ref, k_ref, v_ref, *rest,
                 tile_groups):
    n_w = (len(rest) - 3) // 2
    o_ref, (kz_ref, vz_ref) = rest[n_w], rest[-2:]

    def narrow_weights():
        for w_ref, w16_ref, groups in zip(rest[:n_w], rest[n_w + 1:2 * n_w + 1], tile_groups):
            if groups is None:
                w16_ref[...] = w_ref[...].astype(w16_ref.dtype)
                continue
            tiles = w16_ref.shape[0]
            width = w16_ref.shape[2] // groups
            for tile in range(tiles):
                for g in range(groups):
                    src = (g * tiles + tile) * width
                    w16_ref[tile, :, g * width:(g + 1) * width] = (
                        w_ref[:, src:src + width].astype(w16_ref.dtype))

    blk = ATTN_BLOCK
    nblk = q_ref.shape[0] // blk
    n = pl.program_id(1)
    carry_in = nblk + n % 2
    carry_out = nblk + (n + 1) % 2

    @pl.when(n == 0)
    def _():
        kz_ref[nblk] = jnp.zeros(kz_ref.shape[1:], kz_ref.dtype)
        vz_ref[nblk] = jnp.zeros(vz_ref.shape[1:], vz_ref.dtype)

    ang_t = invf_ref[...] * pos_ref[0]
    q_scale = LOG2E * ATTN_HEAD_DIM ** -0.5

    def spread(tab_t, tile_ref):
        p1 = tab_t.astype(BF16)
        r1 = tab_t - p1.astype(F32)
        p2 = r1.astype(BF16)
        p3 = (r1 - p2.astype(F32)).astype(BF16)
        stacked = jnp.concatenate([p1, p2, p3, jnp.zeros_like(p1)], axis=0)
        return _dot(stacked.astype(F32).T.astype(BF16), tile_ref[...])

    cos_t = jnp.cos(ang_t)
    sin_t = jnp.sin(ang_t)
    cos = spread(cos_t, ctile_ref)
    sin_signed = spread(sin_t, stile_ref)
    cos_q = spread(cos_t * q_scale, ctile_ref)
    sin_q = spread(sin_t * q_scale, stile_ref)

    lane = lax.broadcasted_iota(jnp.int32, (blk, LANES), 1)
    row = lax.broadcasted_iota(jnp.int32, (blk, LANES), 0)
    first_half = (lane & (ATTN_HEAD_DIM // 2)) == 0
    low_head = lane < ATTN_HEAD_DIM
    from_prev = lane > row

    def rope(t, j, c, s):
        rows = slice(j * blk, (j + 1) * blk)
        partner = jnp.where(first_half,
                            pltpu.roll(t, LANES - ATTN_HEAD_DIM // 2, 1),
                            pltpu.roll(t, ATTN_HEAD_DIM // 2, 1))
        return t * c[rows] + partner * s[rows]

    def place(src, dst, j, c):
        slots = (j, carry_out) if j == nblk - 1 else (j,)
        for e in range(2):
            own = jnp.where(low_head if e == 0 else jnp.logical_not(low_head), src, 0.0)
            other = pltpu.roll(own, ATTN_HEAD_DIM, 1)
            lo, hi = (own, other) if e == 0 else (other, own)
            for slot in slots:
                dst[slot, 2 * c + e, 0:blk, 0:LANES] = lo.astype(BF16)
                dst[slot, 2 * c + e, blk:2 * blk, 0:LANES] = hi.astype(BF16)

    def place_keys(j):
        rows = slice(j * blk, (j + 1) * blk)
        for c in range(ATTN_KV_WIDTH // LANES):
            place(rope(k_ref[rows, c * LANES:(c + 1) * LANES].astype(F32), j, cos, sin_signed),
                  kz_ref, j, c)

    def place_values(j):
        rows = slice(j * blk, (j + 1) * blk)
        ones_lo = low_head.astype(BF16)
        for slot in ((j, carry_out) if j == nblk - 1 else (j,)):
            for hk in range(ATTN_KV_HEADS):
                vz_ref[slot, hk, 0:blk, LANES:2 * LANES] = ones_lo
                vz_ref[slot, hk, blk:2 * blk, LANES:2 * LANES] = 1 - ones_lo
        for c in range(ATTN_KV_WIDTH // LANES):
            place(v_ref[rows, c * LANES:(c + 1) * LANES].astype(F32), vz_ref, j, c)

    first_penalty = jnp.where(n > 0, 0.0, MASKED_SCORE)
    items = [(j, hk) for j in range(nblk) for hk in range(ATTN_KV_HEADS)]

    def scores(j, hk):
        rows = slice(j * blk, (j + 1) * blk)
        q = jnp.concatenate(
            [rope(q_ref[rows, g * LANES:(g + 1) * LANES].astype(F32), j, cos_q, sin_q).astype(BF16)
             for g in (2 * hk, 2 * hk + 1)], axis=0)
        prv = carry_in if j == 0 else j - 1
        s_prev, s_cur = _dot_nt(q, kz_ref[prv, hk]), _dot_nt(q, kz_ref[j, hk])
        tiles = []
        for rh in range(2):
            rows = slice(rh * blk, (rh + 1) * blk)
            for ch in range(2):
                cols = slice(ch * blk, (ch + 1) * blk)
                sp = s_prev[rows, cols] + first_penalty if j == 0 else s_prev[rows, cols]
                tiles.append(jnp.where(from_prev, sp, s_cur[rows, cols]))
        return (tiles,)

    def fold_and_max(j, hk, tiles):
        folded = []
        for n_tile, s in enumerate(tiles):
            sink = sink_ref[4 * hk + n_tile] * LOG2E
            folded.append((s, jnp.maximum(jnp.max(s, axis=-1, keepdims=True), sink), sink))
        return folded

    def weigh(j, hk, folded):
        ps, sink_terms = [], []
        for s, m, sink in folded:
            ps.append(jnp.exp2(s - m))
            sink_terms.append(jnp.exp2(sink - m))
        stack = lambda tiles: jnp.concatenate(
            [jnp.concatenate(tiles[0:2], axis=1), jnp.concatenate(tiles[2:4], axis=1)], axis=0)
        p_prev = stack([jnp.where(from_prev, p, 0.0).astype(BF16) for p in ps])
        p_cur = stack([jnp.where(from_prev, 0.0, p).astype(BF16) for p in ps])
        prv = carry_in if j == 0 else j - 1
        acc = _dot(p_prev, vz_ref[prv, hk]) + _dot(p_cur, vz_ref[j, hk])
        return acc, sink_terms

    def finish(j, hk, acc, sink_terms):
        for rh in range(2):
            rows = slice(rh * blk, (rh + 1) * blk)
            g = 2 * hk + rh
            denom = acc[rows, LANES:] + jnp.where(low_head, sink_terms[2 * rh],
                                                  sink_terms[2 * rh + 1])
            o_ref[j * blk:(j + 1) * blk, g * LANES:(g + 1) * LANES] = (
                acc[rows, :LANES] / denom).astype(o_ref.dtype)

    place_keys(0)
    s_vals, f_vals, w_vals = {}, {}, {}
    width = ATTN_ITEMS_PER_SLOT
    for slot in range(len(items) // width + 3):
        for k in range((slot - 3) * width, (slot - 2) * width):
            if 0 <= k < len(items):
                finish(*items[k], *w_vals.pop(k))
        for k in range((slot - 2) * width, (slot - 1) * width):
            if 0 <= k < len(items):
                w_vals[k] = weigh(*items[k], f_vals.pop(k))
        for k in range((slot - 1) * width, slot * width):
            if 0 <= k < len(items):
                f_vals[k] = fold_and_max(*items[k], *s_vals.pop(k))
        for k in range(slot * width, (slot + 1) * width):
            if k < len(items):
                j, hk = items[k]
                s_vals[k] = scores(j, hk)
                if hk == 0:
                    place_values(j)
                if hk == 1 and j + 1 < nblk:
                    place_keys(j + 1)
        if slot == len(items) // width // 2:
            narrow_weights()


def _rope_constants():
    half = ATTN_HEAD_DIM // 2
    inv_freq = ROPE_THETA ** (-np.arange(half, dtype=np.float32) / half)
    k = np.arange(LANES)[:, None]
    lane = np.arange(LANES)[None, :]
    hit = ((k < 3 * half) & (k % half == lane % half)).astype(np.float32)
    sign = np.where(lane % ATTN_HEAD_DIM < half, -1.0, 1.0).astype(np.float32)
    return inv_freq.reshape(half, 1), hit, hit * sign


def _attention(proj, positions, sinks, weights, tilings, batch, seq):
    blk = ATTN_BLOCK
    rows = ATTN_BLOCKS_PER_STEP * blk
    assert seq % rows == 0
    nb = seq // rows
    t = batch * seq
    steps = batch * nb
    w_specs, w16_specs, w16_shapes = [], [], []
    for w, tiling in zip(weights, tilings):
        assert w.shape[0] % (steps * 2 * SUBLANES) == 0, w.shape
        blk_rows = w.shape[0] // steps
        w_specs.append(pl.BlockSpec((blk_rows, w.shape[1]), lambda b, n: (b * nb + n, 0)))
        if tiling is None:
            w16_specs.append(w_specs[-1])
            w16_shapes.append(jax.ShapeDtypeStruct(w.shape, BF16))
        else:
            width = tiling[0] * tiling[1]
            tiles = w.shape[1] // width
            w16_specs.append(pl.BlockSpec((tiles, blk_rows, width), lambda b, n: (0, b * nb + n, 0)))
            w16_shapes.append(jax.ShapeDtypeStruct((tiles, w.shape[0], width), BF16))
    tile_groups = tuple(None if tiling is None else tiling[1] for tiling in tilings)
    inv_freq, ctile, stile = _rope_constants()
    pos_rows = positions.astype(F32).reshape(t // rows, 1, rows)
    const = lambda a: pl.BlockSpec(a.shape, lambda b, n: (0, 0))
    q_blk, k_blk, v_blk = (_block_index(PROJ_OFFSETS[0], ATTN_Q_WIDTH),
                           _block_index(PROJ_OFFSETS[1], ATTN_KV_WIDTH),
                           _block_index(PROJ_OFFSETS[2], ATTN_KV_WIDTH))
    slots = ATTN_BLOCKS_PER_STEP + 2
    k_scratch = pltpu.VMEM((slots, ATTN_KV_HEADS, 2 * blk, LANES), BF16)
    v_scratch = pltpu.VMEM((slots, ATTN_KV_HEADS, 2 * blk, 2 * LANES), BF16)
    outs = pl.pallas_call(
        functools.partial(_attn_kernel, tile_groups=tile_groups),
        grid=(batch, nb),
        in_specs=[pl.BlockSpec(memory_space=pltpu.SMEM),
                  pl.BlockSpec((1, 1, rows), lambda b, n: (b * nb + n, 0, 0)),
                  const(inv_freq), const(ctile), const(stile),
                  pl.BlockSpec((rows, ATTN_Q_WIDTH), lambda b, n: (b * nb + n, q_blk)),
                  pl.BlockSpec((rows, ATTN_KV_WIDTH), lambda b, n: (b * nb + n, k_blk)),
                  pl.BlockSpec((rows, ATTN_KV_WIDTH), lambda b, n: (b * nb + n, v_blk))] + w_specs,
        out_specs=[pl.BlockSpec((rows, ATTN_Q_WIDTH), lambda b, n: (b * nb + n, 0))] + w16_specs,
        out_shape=[jax.ShapeDtypeStruct((t, ATTN_Q_WIDTH), BF16)] + w16_shapes,
        scratch_shapes=[k_scratch, v_scratch],
        compiler_params=pltpu.CompilerParams(
            dimension_semantics=("arbitrary", "arbitrary"),
            vmem_limit_bytes=VMEM_LIMIT_BYTES),
        name="swa_attention",
    )(sinks, pos_rows, jnp.asarray(inv_freq), jnp.asarray(ctile, dtype=BF16),
      jnp.asarray(stile, dtype=BF16), proj, proj, proj, *weights)
    return outs[0], outs[1:]


def _hgrn_constants():
    c = HGRN_CHUNK
    u = np.arange(c)[None, :]
    t = np.arange(c)[:, None]
    ranges, masks = [], []
    for m in HGRN_LEVELS:
        upper = (t & m) != 0
        start = (t // m) * m
        q_rng = (u >= start) & (u <= t)
        k_rng = (u > t) & (u <= start + m - 1)
        if m < SUBLANES:
            ranges.append(np.where(upper, q_rng, k_rng))
        masks.append(upper & ((u & m) == 0) & ((t // (2 * m)) == (u // (2 * m))))
    ranges.append(u <= t)
    masks.append(u == t)
    rng = np.concatenate(ranges, axis=0).astype(np.float32)
    return np.concatenate([rng, rng], axis=1), np.concatenate(masks, axis=0).astype(np.float32)


def _hgrn_kernel(lbl_ref, gn_ref, w_ref, m_ref, q_ref, f_ref, i_ref, g_ref, o_ref, st_ref,
                 *, layer):
    c = HGRN_CHUNK
    nl = len(HGRN_LEVELS)
    n_fine = sum(m < SUBLANES for m in HGRN_LEVELS)
    heads = q_ref.shape[1] // HGRN_DK
    n_chunks = q_ref.shape[0] // c

    @pl.when(pl.program_id(2) == 0)
    def _():
        st_ref[...] = jnp.zeros_like(st_ref)

    lg = lbl_ref[...]
    eg = jnp.exp(lg - jnp.max(lg, axis=0, keepdims=True))
    lb_all = (jnp.sum(eg[0:layer + 1, :], axis=0, keepdims=True)
              / jnp.sum(eg, axis=0, keepdims=True))
    gn = gn_ref[...]

    head_cols = [slice(h * HGRN_DK, (h + 1) * HGRN_DK) for h in range(heads)]

    def gates(ci):
        rows = slice(ci * c, (ci + 1) * c)
        qts, kks, pieces = [], [], []
        for cols in head_cols:
            f_half = 0.5 * (1.0 - lb_all[:, cols])
            f_mid = lb_all[:, cols] + f_half
            z = f_ref[rows, cols].astype(F32)
            qr = q_ref[rows, cols].astype(F32)
            ft = f_half * jnp.tanh(0.5 * z)
            logf = jnp.log2(f_mid + ft)
            kks.append(f_half - ft)
            qh = (0.5 * HGRN_DK ** -0.5) * qr
            qts.append(qh + qh * jnp.tanh(0.5 * qr))
            hi = logf.astype(BF16)
            pieces.append((hi, (logf - hi.astype(F32)).astype(BF16)))
        split = jnp.concatenate(
            [jnp.concatenate([p[0] for p in pieces], axis=1),
             jnp.concatenate([p[1] for p in pieces], axis=1)], axis=0)
        return qts, kks, _dot(w_ref[...], split)

    def coarse_exponent(b, m):
        pieces = []
        for lo in range(0, c, 2 * m):
            ref = b[lo + m - 1:lo + m, :]
            pieces += [ref - b[lo:lo + m, :], b[lo + m:lo + 2 * m, :] - ref]
        return jnp.concatenate(pieces, axis=0)

    def intra(group):
        chains = [(ci, cols, qts[h].astype(BF16), kks[h].astype(BF16), xs,
                   xs[n_fine * c:(n_fine + 1) * c, cols])
                  for ci, qts, kks, xs in group for h, cols in enumerate(head_cols)]
        accs = [m_ref[nl * c:(nl + 1) * c, :] * _dot_nt(q16, k16)
                for _, _, q16, k16, _, _ in chains]
        for li, m in enumerate(HGRN_LEVELS):
            mask = m_ref[li * c:(li + 1) * c, :]
            for n, (_, cols, q16, k16, xs, b) in enumerate(chains):
                x = xs[li * c:(li + 1) * c, cols] if li < n_fine else coarse_exponent(b, m)
                e = jnp.exp2(x).astype(BF16)
                accs[n] = accs[n] + mask * _dot_nt(q16 * e, k16 * e)
        parts = {ci: [] for ci, _, _, _ in group}
        for (ci, cols, q16, k16, _, b), acc in zip(chains, accs):
            eb = jnp.exp2(b)
            el = jnp.exp2(b[c - 1:c, :] - b)
            v = i_ref[ci * c:(ci + 1) * c, cols]
            parts[ci].append((q16 * eb.astype(BF16), _dot(acc.astype(BF16), v), eb[c - 1:c, :],
                              _dot(v.astype(F32).T.astype(BF16), k16 * el.astype(BF16))))
        return parts

    def carry_state(ci, parts, states):
        rows = slice(ci * c, (ci + 1) * c)
        new_states = []
        for h, cols in enumerate(head_cols):
            q_decayed, o_intra, chunk_decay, kv = parts[h]
            st = states[h]
            o = _dot_nt(q_decayed, st.astype(BF16)) + o_intra
            new_states.append(st * chunk_decay + kv)
            gate = g_ref[rows, cols].astype(F32)
            y = o * _rms_scale(o) * gn
            gh = 0.5 * gate
            y = y * (gh + gh * jnp.tanh(gh))
            o_ref[rows, cols] = y.astype(o_ref.dtype)
        return new_states

    states = [st_ref[h] for h in range(heads)]
    gs = HGRN_CHUNKS_PER_GROUP
    groups = [list(range(g0, min(g0 + gs, n_chunks))) for g0 in range(0, n_chunks, gs)]
    g_vals = {0: [(ci,) + gates(ci) for ci in groups[0]]}
    i_vals = {}
    for k in range(len(groups) + 1):
        if k >= 1:
            parts = i_vals.pop(k - 1)
            for ci in groups[k - 1]:
                states = carry_state(ci, parts[ci], states)
        if k + 1 < len(groups):
            g_vals[k + 1] = [(ci,) + gates(ci) for ci in groups[k + 1]]
        if k < len(groups):
            i_vals[k] = intra(g_vals.pop(k))
    for h in range(heads):
        st_ref[h] = states[h]


def _hgrn(proj, lb_logits, gnorm, layer, batch, seq, rows=2048):
    t = batch * seq
    steps = seq // rows
    hp = HGRN_HEADS_PER_STEP
    width = hp * HGRN_DK
    q_blk, f_blk, i_blk, g_blk = (_block_index(PROJ_OFFSETS[s], width) for s in (3, 4, 5, 6))
    rng, masks = _hgrn_constants()
    rng = jnp.asarray(rng, dtype=BF16)
    masks = jnp.asarray(masks, dtype=F32)
    const = lambda a: pl.BlockSpec(a.shape, lambda b, h, s: (0, 0))
    row_spec = lambda base: pl.BlockSpec(
        (rows, width), lambda b, h, s, base=base: (b * steps + s, base + h))
    return pl.pallas_call(
        functools.partial(_hgrn_kernel, layer=layer),
        grid=(batch, HGRN_HEADS // hp, steps),
        in_specs=[pl.BlockSpec((lb_logits.shape[0], width), lambda b, h, s: (0, h)),
                  pl.BlockSpec((1, HGRN_DV), lambda b, h, s: (0, 0)),
                  const(rng), const(masks),
                  row_spec(q_blk), row_spec(f_blk), row_spec(i_blk), row_spec(g_blk)],
        out_specs=pl.BlockSpec((rows, width), lambda b, h, s: (b * steps + s, h)),
        out_shape=jax.ShapeDtypeStruct((t, HGRN_WIDTH), BF16),
        scratch_shapes=[pltpu.VMEM((hp, HGRN_DV, HGRN_DK), F32)],
        compiler_params=pltpu.CompilerParams(
            dimension_semantics=("arbitrary", "arbitrary", "arbitrary")),
        name="hgrn2",
    )(lb_logits, gnorm, rng, masks, proj, proj, proj, proj)


def _merge_kernel(x_ref, ya_ref, yh_ref, wa_ref, wh_ref, wo_ref, gp_ref, *rest):
    n_tiles = (len(rest) - 2) // 2
    ga_refs, gb_refs = rest[:n_tiles], rest[n_tiles:2 * n_tiles]
    o_ref, merged_ref = rest[2 * n_tiles:]
    ya, yh = ya_ref[...], yh_ref[...]
    tile_cols = [slice(c * MXU_SUBTILE, (c + 1) * MXU_SUBTILE) for c in range(n_tiles)]
    for cols, ga_ref, gb_ref in zip(tile_cols, ga_refs, gb_refs):
        merged_ref[:, cols] = (
            _sigmoid(ga_ref[...].astype(F32)) * _dot(ya, wa_ref[:, cols])
            + _sigmoid(gb_ref[...].astype(F32)) * _dot(yh, wh_ref[:, cols])).astype(BF16)
    merged = merged_ref[...]
    for cols in tile_cols:
        o_ref[:, cols] = _dot(merged, wo_ref[:, cols])
    m2 = o_ref[...]
    o_ref[...] = x_ref[...] + m2 * _rms_scale(m2) * gp_ref[...]


def _merge(x2, y_attn, y_hgrn, proj, wa, wh, wo, g_post, tm=512):
    t, d = x2.shape
    n_tiles = d // MXU_SUBTILE
    ga_blk, gb_blk = (_block_index(PROJ_OFFSETS[s], MXU_SUBTILE) for s in (7, 8))
    const = lambda shape: pl.BlockSpec(shape, lambda i: (0, 0), pipeline_mode=pl.Buffered(1))
    gate_specs = [pl.BlockSpec((tm, MXU_SUBTILE), lambda i, blk=base + c: (i, blk))
                  for base in (ga_blk, gb_blk) for c in range(n_tiles)]
    return pl.pallas_call(
        _merge_kernel,
        grid=(t // tm,),
        in_specs=[pl.BlockSpec((tm, d), lambda i: (i, 0)),
                  pl.BlockSpec((tm, ATTN_Q_WIDTH), lambda i: (i, 0)),
                  pl.BlockSpec((tm, HGRN_WIDTH), lambda i: (i, 0)),
                  const(wa.shape), const(wh.shape), const(wo.shape), const((1, d))] + gate_specs,
        out_specs=pl.BlockSpec((tm, d), lambda i: (i, 0)),
        out_shape=jax.ShapeDtypeStruct((t, d), F32),
        scratch_shapes=[pltpu.VMEM((tm, d), BF16)],
        compiler_params=pltpu.CompilerParams(
            dimension_semantics=("arbitrary",), vmem_limit_bytes=VMEM_LIMIT_BYTES),
        name="merge_out",
    )(x2, y_attn, y_hgrn, wa, wh, wo, g_post, *([proj] * (2 * n_tiles)))


def _ffn_kernel(x_ref, gpre_ref, wgu_hbm, wd_hbm, gpost_ref, o_ref, h_ref, wgu_buf, wd_buf, sem):
    i = pl.program_id(0)
    tf = wd_buf.shape[1]
    n_tiles = D_FF // tf
    ring = wgu_buf.shape[0] - 1
    slot_of = lambda j: ring if j == 0 else (j - 1) % ring

    def weight_copies(j):
        slot = slot_of(j)
        return (pltpu.make_async_copy(wgu_hbm.at[j], wgu_buf.at[slot], sem.at[0, slot]),
                pltpu.make_async_copy(wd_hbm.at[pl.ds(j * tf, tf), :], wd_buf.at[slot],
                                      sem.at[1, slot]))

    @pl.when(i == 0)
    def _():
        for copy in weight_copies(0):
            copy.start()

    x = x_ref[...]
    h_ref[...] = (x * _rms_scale(x) * gpre_ref[...]).astype(BF16)
    h = h_ref[...]
    ahead = ring - 1
    for t in range(1, ahead):
        for copy in weight_copies(t):
            copy.start()
    for j in range(n_tiles):
        for copy in weight_copies(j):
            copy.wait()
        slot = slot_of(j)
        gate_up = _dot(h, wgu_buf[slot])
        gate, up = gate_up[:, :tf], gate_up[:, tf:]
        if j + ahead < n_tiles:
            for copy in weight_copies(j + ahead):
                copy.start()
        if j == 1:
            for copy in weight_copies(0):
                copy.start()
        act = (gate * _sigmoid(gate) * up).astype(BF16)
        for c0 in range(0, o_ref.shape[1], FFN_DOWN_SUBTILE):
            cols = slice(c0, c0 + FFN_DOWN_SUBTILE)
            down = _dot(act, wd_buf[slot, :, cols])
            o_ref[:, cols] = down if j == 0 else o_ref[:, cols] + down
    y = o_ref[...]
    o_ref[...] = x_ref[...] + y * _rms_scale(y) * gpost_ref[...]

    @pl.when(i == pl.num_programs(0) - 1)
    def _():
        for copy in weight_copies(0):
            copy.wait()


def _ffn(x1, g_pre, w_gate_up, w_down, g_post, tm=512):
    t, d = x1.shape
    tf = FFN_TILE
    assert w_gate_up.shape == (D_FF // tf, d, 2 * tf)
    slots = 5
    return pl.pallas_call(
        _ffn_kernel,
        grid=(t // tm,),
        in_specs=[pl.BlockSpec((tm, d), lambda i: (i, 0)),
                  pl.BlockSpec((1, d), lambda i: (0, 0)),
                  pl.BlockSpec(memory_space=pl.ANY),
                  pl.BlockSpec(memory_space=pl.ANY),
                  pl.BlockSpec((1, d), lambda i: (0, 0))],
        out_specs=pl.BlockSpec((tm, d), lambda i: (i, 0)),
        out_shape=jax.ShapeDtypeStruct((t, d), F32),
        scratch_shapes=[pltpu.VMEM((tm, d), BF16),
                        pltpu.VMEM((slots, d, 2 * tf), BF16), pltpu.VMEM((slots, tf, d), BF16),
                        pltpu.SemaphoreType.DMA((2, slots))],
        compiler_params=pltpu.CompilerParams(
            dimension_semantics=("arbitrary",), vmem_limit_bytes=VMEM_LIMIT_BYTES),
        name="swiglu_ffn",
    )(x1, g_pre, w_gate_up, w_down, g_post)


def _ple_kernel(x_ref, p_ref, gpre_ref, wg_ref, wp_ref, gpost_ref, o_ref):
    x = x_ref[...]
    h = (x * _rms_scale(x) * gpre_ref[...]).astype(BF16)
    p16 = p_ref[...].astype(BF16)
    for c0 in range(0, o_ref.shape[1], MXU_SUBTILE):
        cols = slice(c0, c0 + MXU_SUBTILE)
        gate = _sigmoid(_dot(h, wg_ref[:, cols]))
        o_ref[:, cols] = _dot(p16, wp_ref[:, cols]) * gate
    e = o_ref[...]
    o_ref[...] = x_ref[...] + e * _rms_scale(e) * gpost_ref[...]


def _ple(x2, p2, g_pre, w_gate, w_proj, g_post, tm=512):
    t, d = x2.shape
    const = lambda shape: pl.BlockSpec(shape, lambda i: (0, 0), pipeline_mode=pl.Buffered(1))
    return pl.pallas_call(
        _ple_kernel,
        grid=(t // tm,),
        in_specs=[pl.BlockSpec((tm, d), lambda i: (i, 0)),
                  pl.BlockSpec((tm, PLE_DIM), lambda i: (i, 0)),
                  const((1, d)), const(w_gate.shape), const(w_proj.shape), const((1, d))],
        out_specs=pl.BlockSpec((tm, d), lambda i: (i, 0)),
        out_shape=jax.ShapeDtypeStruct((t, d), F32),
        compiler_params=pltpu.CompilerParams(
            dimension_semantics=("arbitrary",), vmem_limit_bytes=VMEM_LIMIT_BYTES),
        name="ple",
    )(x2, p2, g_pre, w_gate, w_proj, g_post)


def kernel(x, p, positions, g_mix_pre, w_in, attn_sinks, hgrn_lb_logits, hgrn_gnorm,
           w_attn_branch, w_hgrn_branch, w_out, g_mix_post, g_ffn_pre, w_gate_up, w_down,
           g_ffn_post, g_ple_pre, w_ple_gate, w_ple_proj, g_ple_post):
    batch, seq, d = x.shape
    t = batch * seq
    depth = w_in.shape[0]
    row = lambda v: v.reshape(1, -1)

    xs = x.reshape(t, d)
    for layer in range(depth):
        proj = _inproj(xs, row(g_mix_pre[layer]), w_in[layer].astype(BF16))
        y_attn, (wa, wh, wo, wgu, wd, wpg) = _attention(
            proj, positions, attn_sinks[layer],
            (w_attn_branch[layer], w_hgrn_branch[layer], w_out[layer], w_gate_up[layer],
             w_down[layer], w_ple_gate[layer]),
            (None, None, None, (FFN_TILE, 2), None, None), batch, seq)
        y_hgrn = _hgrn(proj, hgrn_lb_logits, row(hgrn_gnorm[layer]), layer, batch, seq)
        xs = _merge(xs, y_attn, y_hgrn, proj, wa, wh, wo, row(g_mix_post[layer]))
        xs = _ffn(xs, row(g_ffn_pre[layer]), wgu, wd, row(g_ffn_post[layer]))
        xs = _ple(xs, p[layer].reshape(t, PLE_DIM), row(g_ple_pre[layer]), wpg,
                  w_ple_proj[layer].astype(BF16), row(g_ple_post[layer]))
    return xs.reshape(batch, seq, d)
```

```python
import functools
import math

import numpy as np
import jax
import jax.numpy as jnp
from jax import lax
from jax.experimental import pallas as pl
from jax.experimental.pallas import tpu as pltpu

D_MODEL = 2048
PLE_DIM = 256
ATTN_HEADS = 16
ATTN_KV_HEADS = 4
ATTN_HEAD_DIM = 64
ATTN_Q_WIDTH = ATTN_HEADS * ATTN_HEAD_DIM
ATTN_KV_WIDTH = ATTN_KV_HEADS * ATTN_HEAD_DIM
ATTN_BLOCK = 128
ATTN_BLOCKS_PER_STEP = 4
ATTN_ITEMS_PER_SLOT = 8
ROPE_THETA = 10000.0
HGRN_HEADS = 8
HGRN_DK = 128
HGRN_DV = 128
HGRN_WIDTH = HGRN_HEADS * HGRN_DK
D_FF = 5632
RMS_EPS = 1e-6
IN_SIZES = (ATTN_Q_WIDTH, ATTN_KV_WIDTH, ATTN_KV_WIDTH,
            HGRN_WIDTH, HGRN_WIDTH, HGRN_WIDTH, HGRN_WIDTH, D_MODEL, D_MODEL)
IN_WIDTH = sum(IN_SIZES)
PROJ_OFFSETS = tuple(int(v) for v in np.cumsum((0,) + IN_SIZES[:-1]))

LANES = 128
SUBLANES = 8
HGRN_CHUNK = 128
HGRN_LEVELS = (1, 2, 4, 8, 16, 32, 64)
HGRN_CHUNKS_PER_GROUP = 2
HGRN_HEADS_PER_STEP = 4
MXU_SUBTILE = 512
FFN_TILE = 512
FFN_DOWN_SUBTILE = 1024
INPROJ_SUBTILE = 1024
VMEM_LIMIT_BYTES = 58 * 1024 * 1024
LOG2E = math.log2(math.e)
MASKED_SCORE = -1e30

BF16 = jnp.bfloat16
F32 = jnp.float32


def _rms_scale(v):
    return lax.rsqrt(jnp.mean(v * v, axis=-1, keepdims=True) + RMS_EPS)


def _sigmoid(v):
    return 0.5 * jnp.tanh(0.5 * v) + 0.5


def _dot(a, b):
    return jnp.dot(a, b, preferred_element_type=F32)


def _block_index(offset, width):
    assert offset % width == 0, (offset, width)
    return offset // width


def _dot_nt(a, b):
    return lax.dot_general(a, b, (((1,), (1,)), ((), ())), preferred_element_type=F32)


def _inproj_kernel(x_ref, g_ref, w_ref, o_ref, h_ref):
    @pl.when(pl.program_id(1) == 0)
    def _():
        x = x_ref[...]
        h_ref[...] = (x * _rms_scale(x) * g_ref[...]).astype(BF16)

    for c0 in range(0, o_ref.shape[1], INPROJ_SUBTILE):
        cols = slice(c0, min(c0 + INPROJ_SUBTILE, o_ref.shape[1]))
        o_ref[:, cols] = _dot(h_ref[...], w_ref[:, cols]).astype(o_ref.dtype)


def _inproj(x2, g, w, tm=1024, tn=2432):
    t, d = x2.shape
    n = w.shape[1]
    assert t % tm == 0 and n % tn == 0
    return pl.pallas_call(
        _inproj_kernel,
        grid=(t // tm, n // tn),
        in_specs=[pl.BlockSpec((tm, d), lambda i, j: (i, 0)),
                  pl.BlockSpec((1, d), lambda i, j: (0, 0)),
                  pl.BlockSpec((d, tn), lambda i, j: (0, j))],
        out_specs=pl.BlockSpec((tm, tn), lambda i, j: (i, j)),
        out_shape=jax.ShapeDtypeStruct((t, n), BF16),
        scratch_shapes=[pltpu.VMEM((tm, d), BF16)],
        compiler_params=pltpu.CompilerParams(
            dimension_semantics=("arbitrary", "arbitrary"),
            vmem_limit_bytes=VMEM_LIMIT_BYTES),
        name="in_proj",
    )(x2, g, w)


def _attn_kernel(sink_ref, pos_ref, invf_ref, ctile_ref, stile_ref, q_ref, k_ref, v_ref, *rest,
                 tile_groups):
    n_w = (len(rest) - 3) // 2
    o_ref, (kz_ref, vz_ref) = rest[n_w], rest[-2:]

    def narrow_weights():
        for w_ref, w16_ref, groups in zip(rest[:n_w], rest[n_w + 1:2 * n_w + 1], tile_groups):
            if groups is None:
                w16_ref[...] = w_ref[...].astype(w16_ref.dtype)
                continue
            tiles = w16_ref.shape[0]
            width = w16_ref.shape[2] // groups
            for tile in range(tiles):
                for g in range(groups):
                    src = (g * tiles + tile) * width
                    w16_ref[tile, :, g * width:(g + 1) * width] = (
                        w_ref[:, src:src + width].astype(w16_ref.dtype))

    blk = ATTN_BLOCK
    nblk = q_ref.shape[0] // blk
    n = pl.program_id(1)
    carry_in = nblk + n % 2
    carry_out = nblk + (n + 1) % 2

    @pl.when(n == 0)
    def _():
        kz_ref[nblk] = jnp.zeros(kz_ref.shape[1:], kz_ref.dtype)
        vz_ref[nblk] = jnp.zeros(vz_ref.shape[1:], vz_ref.dtype)

    ang_t = invf_ref[...] * pos_ref[0]
    q_scale = LOG2E * ATTN_HEAD_DIM ** -0.5

    def spread(tab_t, tile_ref):
        p1 = tab_t.astype(BF16)
        r1 = tab_t - p1.astype(F32)
        p2 = r1.astype(BF16)
        p3 = (r1 - p2.astype(F32)).astype(BF16)
        stacked = jnp.concatenate([p1, p2, p3, jnp.zeros_like(p1)], axis=0)
        return _dot(stacked.astype(F32).T.astype(BF16), tile_ref[...])

    cos_t = jnp.cos(ang_t)
    sin_t = jnp.sin(ang_t)
    cos = spread(cos_t, ctile_ref)
    sin_signed = spread(sin_t, stile_ref)
    cos_q = spread(cos_t * q_scale, ctile_ref)
    sin_q = spread(sin_t * q_scale, stile_ref)

    lane = lax.broadcasted_iota(jnp.int32, (blk, LANES), 1)
    row = lax.broadcasted_iota(jnp.int32, (blk, LANES), 0)
    first_half = (lane & (ATTN_HEAD_DIM // 2)) == 0
    low_head = lane < ATTN_HEAD_DIM
    from_prev = lane > row

    def rope(t, j, c, s):
        rows = slice(j * blk, (j + 1) * blk)
        partner = jnp.where(first_half,
                            pltpu.roll(t, LANES - ATTN_HEAD_DIM // 2, 1),
                            pltpu.roll(t, ATTN_HEAD_DIM // 2, 1))
        return t * c[rows] + partner * s[rows]

    def place(src, dst, j, c):
        slots = (j, carry_out) if j == nblk - 1 else (j,)
        for e in range(2):
            own = jnp.where(low_head if e == 0 else jnp.logical_not(low_head), src, 0.0)
            other = pltpu.roll(own, ATTN_HEAD_DIM, 1)
            lo, hi = (own, other) if e == 0 else (other, own)
            for slot in slots:
                dst[slot, 2 * c + e, 0:blk, 0:LANES] = lo.astype(BF16)
                dst[slot, 2 * c + e, blk:2 * blk, 0:LANES] = hi.astype(BF16)

    def place_keys(j):
        rows = slice(j * blk, (j + 1) * blk)
        for c in range(ATTN_KV_WIDTH // LANES):
            place(rope(k_ref[rows, c * LANES:(c + 1) * LANES].astype(F32), j, cos, sin_signed),
                  kz_ref, j, c)

    def place_values(j):
        rows = slice(j * blk, (j + 1) * blk)
        ones_lo = low_head.astype(BF16)
        for slot in ((j, carry_out) if j == nblk - 1 else (j,)):
            for hk in range(ATTN_KV_HEADS):
                vz_ref[slot, hk, 0:blk, LANES:2 * LANES] = ones_lo
                vz_ref[slot, hk, blk:2 * blk, LANES:2 * LANES] = 1 - ones_lo
        for c in range(ATTN_KV_WIDTH // LANES):
            place(v_ref[rows, c * LANES:(c + 1) * LANES].astype(F32), vz_ref, j, c)

    first_penalty = jnp.where(n > 0, 0.0, MASKED_SCORE)
    items = [(j, hk) for j in range(nblk) for hk in range(ATTN_KV_HEADS)]

    def scores(j, hk):
        rows = slice(j * blk, (j + 1) * blk)
        q = jnp.concatenate(
            [rope(q_ref[rows, g * LANES:(g + 1) * LANES].astype(F32), j, cos_q, sin_q).astype(BF16)
             for g in (2 * hk, 2 * hk + 1)], axis=0)
        prv = carry_in if j == 0 else j - 1
        s_prev, s_cur = _dot_nt(q, kz_ref[prv, hk]), _dot_nt(q, kz_ref[j, hk])
        tiles = []
        for rh in range(2):
            rows = slice(rh * blk, (rh + 1) * blk)
            for ch in range(2):
                cols = slice(ch * blk, (ch + 1) * blk)
                sp = s_prev[rows, cols] + first_penalty if j == 0 else s_prev[rows, cols]
                tiles.append(jnp.where(from_prev, sp, s_cur[rows, cols]))
        return (tiles,)

    def fold_and_max(j, hk, tiles):
        folded = []
        for n_tile, s in enumerate(tiles):
            sink = sink_ref[4 * hk + n_tile] * LOG2E
            folded.append((s, jnp.maximum(jnp.max(s, axis=-1, keepdims=True), sink), sink))
        return folded

    def weigh(j, hk, folded):
        ps, sink_terms = [], []
        for s, m, sink in folded:
            ps.append(jnp.exp2(s - m))
            sink_terms.append(jnp.exp2(sink - m))
        stack = lambda tiles: jnp.concatenate(
            [jnp.concatenate(tiles[0:2], axis=1), jnp.concatenate(tiles[2:4], axis=1)], axis=0)
        p_prev = stack([jnp.where(from_prev, p, 0.0).astype(BF16) for p in ps])
        p_cur = stack([jnp.where(from_prev, 0.0, p).astype(BF16) for p in ps])
        prv = carry_in if j == 0 else j - 1
        acc = _dot(p_prev, vz_ref[prv, hk]) + _dot(p_cur, vz_ref[j, hk])
        return acc, sink_terms

    def finish(j, hk, acc, sink_terms):
        for rh in range(2):
            rows = slice(rh * blk, (rh + 1) * blk)
            g = 2 * hk + rh
            denom = acc[rows, LANES:] + jnp.where(low_head, sink_terms[2 * rh],
                                                  sink_terms[2 * rh + 1])
            o_ref[j * blk:(j + 1) * blk, g * LANES:(g + 1) * LANES] = (
                acc[rows, :LANES] / denom).astype(o_ref.dtype)

    place_keys(0)
    s_vals, f_vals, w_vals = {}, {}, {}
    width = ATTN_ITEMS_PER_SLOT
    for slot in range(len(items) // width + 3):
        for k in range((slot - 3) * width, (slot - 2) * width):
            if 0 <= k < len(items):
                finish(*items[k], *w_vals.pop(k))
        for k in range((slot - 2) * width, (slot - 1) * width):
            if 0 <= k < len(items):
                w_vals[k] = weigh(*items[k], f_vals.pop(k))
        for k in range((slot - 1) * width, slot * width):
            if 0 <= k < len(items):
                f_vals[k] = fold_and_max(*items[k], *s_vals.pop(k))
        for k in range(slot * width, (slot + 1) * width):
            if k < len(items):
                j, hk = items[k]
                s_vals[k] = scores(j, hk)
                if hk == 0:
                    place_values(j)
                if hk == 1 and j + 1 < nblk:
                    place_keys(j + 1)
        if slot == len(items) // width // 2:
            narrow_weights()


def _rope_constants():
    half = ATTN_HEAD_DIM // 2
    inv_freq = ROPE_THETA ** (-np.arange(half, dtype=np.float32) / half)
    k = np.arange(LANES)[:, None]
    lane = np.arange(LANES)[None, :]
    hit = ((k < 3 * half) & (k % half == lane % half)).astype(np.float32)
    sign = np.where(lane % ATTN_HEAD_DIM < half, -1.0, 1.0).astype(np.float32)
    return inv_freq.reshape(half, 1), hit, hit * sign


def _attention(proj, positions, sinks, weights, tilings, batch, seq):
    blk = ATTN_BLOCK
    rows = ATTN_BLOCKS_PER_STEP * blk
    assert seq % rows == 0
    nb = seq // rows
    t = batch * seq
    steps = batch * nb
    w_specs, w16_specs, w16_shapes = [], [], []
    for w, tiling in zip(weights, tilings):
        assert w.shape[0] % (steps * 2 * SUBLANES) == 0, w.shape
        blk_rows = w.shape[0] // steps
        w_specs.append(pl.BlockSpec((blk_rows, w.shape[1]), lambda b, n: (b * nb + n, 0)))
        if tiling is None:
            w16_specs.append(w_specs[-1])
            w16_shapes.append(jax.ShapeDtypeStruct(w.shape, BF16))
        else:
            width = tiling[0] * tiling[1]
            tiles = w.shape[1] // width
            w16_specs.append(pl.BlockSpec((tiles, blk_rows, width), lambda b, n: (0, b * nb + n, 0)))
            w16_shapes.append(jax.ShapeDtypeStruct((tiles, w.shape[0], width), BF16))
    tile_groups = tuple(None if tiling is None else tiling[1] for tiling in tilings)
    inv_freq, ctile, stile = _rope_constants()
    pos_rows = positions.astype(F32).reshape(t // rows, 1, rows)
    const = lambda a: pl.BlockSpec(a.shape, lambda b, n: (0, 0))
    q_blk, k_blk, v_blk = (_block_index(PROJ_OFFSETS[0], ATTN_Q_WIDTH),
                           _block_index(PROJ_OFFSETS[1], ATTN_KV_WIDTH),
                           _block_index(PROJ_OFFSETS[2], ATTN_KV_WIDTH))
    slots = ATTN_BLOCKS_PER_STEP + 2
    k_scratch = pltpu.VMEM((slots, ATTN_KV_HEADS, 2 * blk, LANES), BF16)
    v_scratch = pltpu.VMEM((slots, ATTN_KV_HEADS, 2 * blk, 2 * LANES), BF16)
    outs = pl.pallas_call(
        functools.partial(_attn_kernel, tile_groups=tile_groups),
        grid=(batch, nb),
        in_specs=[pl.BlockSpec(memory_space=pltpu.SMEM),
                  pl.BlockSpec((1, 1, rows), lambda b, n: (b * nb + n, 0, 0)),
                  const(inv_freq), const(ctile), const(stile),
                  pl.BlockSpec((rows, ATTN_Q_WIDTH), lambda b, n: (b * nb + n, q_blk)),
                  pl.BlockSpec((rows, ATTN_KV_WIDTH), lambda b, n: (b * nb + n, k_blk)),
                  pl.BlockSpec((rows, ATTN_KV_WIDTH), lambda b, n: (b * nb + n, v_blk))] + w_specs,
        out_specs=[pl.BlockSpec((rows, ATTN_Q_WIDTH), lambda b, n: (b * nb + n, 0))] + w16_specs,
        out_shape=[jax.ShapeDtypeStruct((t, ATTN_Q_WIDTH), BF16)] + w16_shapes,
        scratch_shapes=[k_scratch, v_scratch],
        compiler_params=pltpu.CompilerParams(
            dimension_semantics=("arbitrary", "arbitrary"),
            vmem_limit_bytes=VMEM_LIMIT_BYTES),
        name="swa_attention",
    )(sinks, pos_rows, jnp.asarray(inv_freq), jnp.asarray(ctile, dtype=BF16),
      jnp.asarray(stile, dtype=BF16), proj, proj, proj, *weights)
    return outs[0], outs[1:]


def _hgrn_constants():
    c = HGRN_CHUNK
    u = np.arange(c)[None, :]
    t = np.arange(c)[:, None]
    ranges, masks = [], []
    for m in HGRN_LEVELS:
        upper = (t & m) != 0
        start = (t // m) * m
        q_rng = (u >= start) & (u <= t)
        k_rng = (u > t) & (u <= start + m - 1)
        if m < SUBLANES:
            ranges.append(np.where(upper, q_rng, k_rng))
        masks.append(upper & ((u & m) == 0) & ((t // (2 * m)) == (u // (2 * m))))
    ranges.append(u <= t)
    masks.append(u == t)
    rng = np.concatenate(ranges, axis=0).astype(np.float32)
    return np.concatenate([rng, rng], axis=1), np.concatenate(masks, axis=0).astype(np.float32)


def _hgrn_kernel(lbl_ref, gn_ref, w_ref, m_ref, q_ref, f_ref, i_ref, g_ref, o_ref, st_ref,
                 *, layer):
    c = HGRN_CHUNK
    nl = len(HGRN_LEVELS)
    n_fine = sum(m < SUBLANES for m in HGRN_LEVELS)
    heads = q_ref.shape[1] // HGRN_DK
    n_chunks = q_ref.shape[0] // c

    @pl.when(pl.program_id(2) == 0)
    def _():
        st_ref[...] = jnp.zeros_like(st_ref)

    lg = lbl_ref[...]
    eg = jnp.exp(lg - jnp.max(lg, axis=0, keepdims=True))
    lb_all = (jnp.sum(eg[0:layer + 1, :], axis=0, keepdims=True)
              / jnp.sum(eg, axis=0, keepdims=True))
    gn = gn_ref[...]

    head_cols = [slice(h * HGRN_DK, (h + 1) * HGRN_DK) for h in range(heads)]

    def gates(ci):
        rows = slice(ci * c, (ci + 1) * c)
        qts, kks, pieces = [], [], []
        for cols in head_cols:
            f_half = 0.5 * (1.0 - lb_all[:, cols])
            f_mid = lb_all[:, cols] + f_half
            z = f_ref[rows, cols].astype(F32)
            qr = q_ref[rows, cols].astype(F32)
            ft = f_half * jnp.tanh(0.5 * z)
            logf = jnp.log2(f_mid + ft)
            kks.append(f_half - ft)
            qh = (0.5 * HGRN_DK ** -0.5) * qr
            qts.append(qh + qh * jnp.tanh(0.5 * qr))
            hi = logf.astype(BF16)
            pieces.append((hi, (logf - hi.astype(F32)).astype(BF16)))
        split = jnp.concatenate(
            [jnp.concatenate([p[0] for p in pieces], axis=1),
             jnp.concatenate([p[1] for p in pieces], axis=1)], axis=0)
        return qts, kks, _dot(w_ref[...], split)

    def coarse_exponent(b, m):
        pieces = []
        for lo in range(0, c, 2 * m):
            ref = b[lo + m - 1:lo + m, :]
            pieces += [ref - b[lo:lo + m, :], b[lo + m:lo + 2 * m, :] - ref]
        return jnp.concatenate(pieces, axis=0)

    def intra(group):
        chains = [(ci, cols, qts[h].astype(BF16), kks[h].astype(BF16), xs,
                   xs[n_fine * c:(n_fine + 1) * c, cols])
                  for ci, qts, kks, xs in group for h, cols in enumerate(head_cols)]
        accs = [m_ref[nl * c:(nl + 1) * c, :] * _dot_nt(q16, k16)
                for _, _, q16, k16, _, _ in chains]
        for li, m in enumerate(HGRN_LEVELS):
            mask = m_ref[li * c:(li + 1) * c, :]
            for n, (_, cols, q16, k16, xs, b) in enumerate(chains):
                x = xs[li * c:(li + 1) * c, cols] if li < n_fine else coarse_exponent(b, m)
                e = jnp.exp2(x).astype(BF16)
                accs[n] = accs[n] + mask * _dot_nt(q16 * e, k16 * e)
        parts = {ci: [] for ci, _, _, _ in group}
        for (ci, cols, q16, k16, _, b), acc in zip(chains, accs):
            eb = jnp.exp2(b)
            el = jnp.exp2(b[c - 1:c, :] - b)
            v = i_ref[ci * c:(ci + 1) * c, cols]
            parts[ci].append((q16 * eb.astype(BF16), _dot(acc.astype(BF16), v), eb[c - 1:c, :],
                              _dot(v.astype(F32).T.astype(BF16), k16 * el.astype(BF16))))
        return parts

    def carry_state(ci, parts, states):
        rows = slice(ci * c, (ci + 1) * c)
        new_states = []
        for h, cols in enumerate(head_cols):
            q_decayed, o_intra, chunk_decay, kv = parts[h]
            st = states[h]
            o = _dot_nt(q_decayed, st.astype(BF16)) + o_intra
            new_states.append(st * chunk_decay + kv)
            gate = g_ref[rows, cols].astype(F32)
            y = o * _rms_scale(o) * gn
            gh = 0.5 * gate
            y = y * (gh + gh * jnp.tanh(gh))
            o_ref[rows, cols] = y.astype(o_ref.dtype)
        return new_states

    states = [st_ref[h] for h in range(heads)]
    gs = HGRN_CHUNKS_PER_GROUP
    groups = [list(range(g0, min(g0 + gs, n_chunks))) for g0 in range(0, n_chunks, gs)]
    g_vals = {0: [(ci,) + gates(ci) for ci in groups[0]]}
    i_vals = {}
    for k in range(len(groups) + 1):
        if k >= 1:
            parts = i_vals.pop(k - 1)
            for ci in groups[k - 1]:
                states = carry_state(ci, parts[ci], states)
        if k < len(groups):
            i_vals[k] = intra(g_vals.pop(k))
        if k + 1 < len(groups):
            g_vals[k + 1] = [(ci,) + gates(ci) for ci in groups[k + 1]]
    for h in range(heads):
        st_ref[h] = states[h]


def _hgrn(proj, lb_logits, gnorm, layer, batch, seq, rows=1024):
    t = batch * seq
    steps = seq // rows
    hp = HGRN_HEADS_PER_STEP
    width = hp * HGRN_DK
    q_blk, f_blk, i_blk, g_blk = (_block_index(PROJ_OFFSETS[s], width) for s in (3, 4, 5, 6))
    rng, masks = _hgrn_constants()
    rng = jnp.asarray(rng, dtype=BF16)
    masks = jnp.asarray(masks, dtype=F32)
    const = lambda a: pl.BlockSpec(a.shape, lambda b, h, s: (0, 0))
    row_spec = lambda base: pl.BlockSpec(
        (rows, width), lambda b, h, s, base=base: (b * steps + s, base + h))
    return pl.pallas_call(
        functools.partial(_hgrn_kernel, layer=layer),
        grid=(batch, HGRN_HEADS // hp, steps),
        in_specs=[pl.BlockSpec((lb_logits.shape[0], width), lambda b, h, s: (0, h)),
                  pl.BlockSpec((1, HGRN_DV), lambda b, h, s: (0, 0)),
                  const(rng), const(masks),
                  row_spec(q_blk), row_spec(f_blk), row_spec(i_blk), row_spec(g_blk)],
        out_specs=pl.BlockSpec((rows, width), lambda b, h, s: (b * steps + s, h)),
        out_shape=jax.ShapeDtypeStruct((t, HGRN_WIDTH), BF16),
        scratch_shapes=[pltpu.VMEM((hp, HGRN_DV, HGRN_DK), F32)],
        compiler_params=pltpu.CompilerParams(
            dimension_semantics=("arbitrary", "arbitrary", "arbitrary")),
        name="hgrn2",
    )(lb_logits, gnorm, rng, masks, proj, proj, proj, proj)


def _merge_kernel(x_ref, ya_ref, yh_ref, wa_ref, wh_ref, wo_ref, gp_ref, *rest):
    n_tiles = (len(rest) - 2) // 2
    ga_refs, gb_refs = rest[:n_tiles], rest[n_tiles:2 * n_tiles]
    o_ref, merged_ref = rest[2 * n_tiles:]
    ya, yh = ya_ref[...], yh_ref[...]
    tile_cols = [slice(c * MXU_SUBTILE, (c + 1) * MXU_SUBTILE) for c in range(n_tiles)]
    for cols, ga_ref, gb_ref in zip(tile_cols, ga_refs, gb_refs):
        merged_ref[:, cols] = (
            _sigmoid(ga_ref[...].astype(F32)) * _dot(ya, wa_ref[:, cols])
            + _sigmoid(gb_ref[...].astype(F32)) * _dot(yh, wh_ref[:, cols])).astype(BF16)
    merged = merged_ref[...]
    for cols in tile_cols:
        o_ref[:, cols] = _dot(merged, wo_ref[:, cols])
    m2 = o_ref[...]
    o_ref[...] = x_ref[...] + m2 * _rms_scale(m2) * gp_ref[...]


def _merge(x2, y_attn, y_hgrn, proj, wa, wh, wo, g_post, tm=512):
    t, d = x2.shape
    n_tiles = d // MXU_SUBTILE
    ga_blk, gb_blk = (_block_index(PROJ_OFFSETS[s], MXU_SUBTILE) for s in (7, 8))
    const = lambda shape: pl.BlockSpec(shape, lambda i: (0, 0), pipeline_mode=pl.Buffered(1))
    gate_specs = [pl.BlockSpec((tm, MXU_SUBTILE), lambda i, blk=base + c: (i, blk))
                  for base in (ga_blk, gb_blk) for c in range(n_tiles)]
    return pl.pallas_call(
        _merge_kernel,
        grid=(t // tm,),
        in_specs=[pl.BlockSpec((tm, d), lambda i: (i, 0)),
                  pl.BlockSpec((tm, ATTN_Q_WIDTH), lambda i: (i, 0)),
                  pl.BlockSpec((tm, HGRN_WIDTH), lambda i: (i, 0)),
                  const(wa.shape), const(wh.shape), const(wo.shape), const((1, d))] + gate_specs,
        out_specs=pl.BlockSpec((tm, d), lambda i: (i, 0)),
        out_shape=jax.ShapeDtypeStruct((t, d), F32),
        scratch_shapes=[pltpu.VMEM((tm, d), BF16)],
        compiler_params=pltpu.CompilerParams(
            dimension_semantics=("arbitrary",), vmem_limit_bytes=VMEM_LIMIT_BYTES),
        name="merge_out",
    )(x2, y_attn, y_hgrn, wa, wh, wo, g_post, *([proj] * (2 * n_tiles)))


def _ffn_kernel(x_ref, gpre_ref, wgu_hbm, wd_hbm, gpost_ref, o_ref, h_ref, wgu_buf, wd_buf, sem):
    i = pl.program_id(0)
    tf = wd_buf.shape[1]
    n_tiles = D_FF // tf
    ring = wgu_buf.shape[0] - 1
    slot_of = lambda j: ring if j == 0 else (j - 1) % ring

    def weight_copies(j):
        slot = slot_of(j)
        return (pltpu.make_async_copy(wgu_hbm.at[j], wgu_buf.at[slot], sem.at[0, slot]),
                pltpu.make_async_copy(wd_hbm.at[pl.ds(j * tf, tf), :], wd_buf.at[slot],
                                      sem.at[1, slot]))

    @pl.when(i == 0)
    def _():
        for copy in weight_copies(0):
            copy.start()

    x = x_ref[...]
    h_ref[...] = (x * _rms_scale(x) * gpre_ref[...]).astype(BF16)
    h = h_ref[...]
    ahead = ring - 1
    for t in range(1, ahead):
        for copy in weight_copies(t):
            copy.start()
    for j in range(n_tiles):
        for copy in weight_copies(j):
            copy.wait()
        slot = slot_of(j)
        gate_up = _dot(h, wgu_buf[slot])
        gate, up = gate_up[:, :tf], gate_up[:, tf:]
        if j + ahead < n_tiles:
            for copy in weight_copies(j + ahead):
                copy.start()
        if j == 1:
            for copy in weight_copies(0):
                copy.start()
        act = (gate * _sigmoid(gate) * up).astype(BF16)
        for c0 in range(0, o_ref.shape[1], FFN_DOWN_SUBTILE):
            cols = slice(c0, c0 + FFN_DOWN_SUBTILE)
            down = _dot(act, wd_buf[slot, :, cols])
            o_ref[:, cols] = down if j == 0 else o_ref[:, cols] + down
    y = o_ref[...]
    o_ref[...] = x_ref[...] + y * _rms_scale(y) * gpost_ref[...]

    @pl.when(i == pl.num_programs(0) - 1)
    def _():
        for copy in weight_copies(0):
            copy.wait()


def _ffn(x1, g_pre, w_gate_up, w_down, g_post, tm=512):
    t, d = x1.shape
    tf = FFN_TILE
    assert w_gate_up.shape == (D_FF // tf, d, 2 * tf)
    slots = 5
    return pl.pallas_call(
        _ffn_kernel,
        grid=(t // tm,),
        in_specs=[pl.BlockSpec((tm, d), lambda i: (i, 0)),
                  pl.BlockSpec((1, d), lambda i: (0, 0)),
                  pl.BlockSpec(memory_space=pl.ANY),
                  pl.BlockSpec(memory_space=pl.ANY),
                  pl.BlockSpec((1, d), lambda i: (0, 0))],
        out_specs=pl.BlockSpec((tm, d), lambda i: (i, 0)),
        out_shape=jax.ShapeDtypeStruct((t, d), F32),
        scratch_shapes=[pltpu.VMEM((tm, d), BF16),
                        pltpu.VMEM((slots, d, 2 * tf), BF16), pltpu.VMEM((slots, tf, d), BF16),
                        pltpu.SemaphoreType.DMA((2, slots))],
        compiler_params=pltpu.CompilerParams(
            dimension_semantics=("arbitrary",), vmem_limit_bytes=VMEM_LIMIT_BYTES),
        name="swiglu_ffn",
    )(x1, g_pre, w_gate_up, w_down, g_post)


def _ple_kernel(x_ref, p_ref, gpre_ref, wg_ref, wp_ref, gpost_ref, o_ref):
    x = x_ref[...]
    h = (x * _rms_scale(x) * gpre_ref[...]).astype(BF16)
    p16 = p_ref[...].astype(BF16)
    for c0 in range(0, o_ref.shape[1], MXU_SUBTILE):
        cols = slice(c0, c0 + MXU_SUBTILE)
        gate = _sigmoid(_dot(h, wg_ref[:, cols]))
        o_ref[:, cols] = _dot(p16, wp_ref[:, cols]) * gate
    e = o_ref[...]
    o_ref[...] = x_ref[...] + e * _rms_scale(e) * gpost_ref[...]


def _ple(x2, p2, g_pre, w_gate, w_proj, g_post, tm=512):
    t, d = x2.shape
    const = lambda shape: pl.BlockSpec(shape, lambda i: (0, 0), pipeline_mode=pl.Buffered(1))
    return pl.pallas_call(
        _ple_kernel,
        grid=(t // tm,),
        in_specs=[pl.BlockSpec((tm, d), lambda i: (i, 0)),
                  pl.BlockSpec((tm, PLE_DIM), lambda i: (i, 0)),
                  const((1, d)), const(w_gate.shape), const(w_proj.shape), const((1, d))],
        out_specs=pl.BlockSpec((tm, d), lambda i: (i, 0)),
        out_shape=jax.ShapeDtypeStruct((t, d), F32),
        compiler_params=pltpu.CompilerParams(
            dimension_semantics=("arbitrary",), vmem_limit_bytes=VMEM_LIMIT_BYTES),
        name="ple",
    )(x2, p2, g_pre, w_gate, w_proj, g_post)


def kernel(x, p, positions, g_mix_pre, w_in, attn_sinks, hgrn_lb_logits, hgrn_gnorm,
           w_attn_branch, w_hgrn_branch, w_out, g_mix_post, g_ffn_pre, w_gate_up, w_down,
           g_ffn_post, g_ple_pre, w_ple_gate, w_ple_proj, g_ple_post):
    batch, seq, d = x.shape
    t = batch * seq
    depth = w_in.shape[0]
    row = lambda v: v.reshape(1, -1)

    xs = x.reshape(t, d)
    for layer in range(depth):
        proj = _inproj(xs, row(g_mix_pre[layer]), w_in[layer].astype(BF16))
        y_attn, (wa, wh, wo, wgu, wd, wpg) = _attention(
            proj, positions, attn_sinks[layer],
            (w_attn_branch[layer], w_hgrn_branch[layer], w_out[layer], w_gate_up[layer],
             w_down[layer], w_ple_gate[layer]),
            (None, None, None, (FFN_TILE, 2), None, None), batch, seq)
        y_hgrn = _hgrn(proj, hgrn_lb_logits, row(hgrn_gnorm[layer]), layer, batch, seq)
        xs = _merge(xs, y_attn, y_hgrn, proj, wa, wh, wo, row(g_mix_post[layer]))
        xs = _ffn(xs, row(g_ffn_pre[layer]), wgu, wd, row(g_ffn_post[layer]))
        xs = _ple(xs, p[layer].reshape(t, PLE_DIM), row(g_ple_pre[layer]), wpg,
                  w_ple_proj[layer].astype(BF16), row(g_ple_post[layer]))
    return xs.reshape(batch, seq, d)
```

```python
import functools
import math

import numpy as np
import jax
import jax.numpy as jnp
from jax import lax
from jax.experimental import pallas as pl
from jax.experimental.pallas import tpu as pltpu

D_MODEL = 2048
PLE_DIM = 256
ATTN_HEADS = 16
ATTN_KV_HEADS = 4
ATTN_HEAD_DIM = 64
ATTN_Q_WIDTH = ATTN_HEADS * ATTN_HEAD_DIM
ATTN_KV_WIDTH = ATTN_KV_HEADS * ATTN_HEAD_DIM
ATTN_BLOCK = 128
ATTN_BLOCKS_PER_STEP = 4
ATTN_ITEMS_PER_SLOT = 8
ROPE_THETA = 10000.0
HGRN_HEADS = 8
HGRN_DK = 128
HGRN_DV = 128
HGRN_WIDTH = HGRN_HEADS * HGRN_DK
D_FF = 5632
RMS_EPS = 1e-6
IN_SIZES = (ATTN_Q_WIDTH, ATTN_KV_WIDTH, ATTN_KV_WIDTH,
            HGRN_WIDTH, HGRN_WIDTH, HGRN_WIDTH, HGRN_WIDTH, D_MODEL, D_MODEL)
IN_WIDTH = sum(IN_SIZES)
PROJ_OFFSETS = tuple(int(v) for v in np.cumsum((0,) + IN_SIZES[:-1]))

LANES = 128
SUBLANES = 8
HGRN_CHUNK = 128
HGRN_LEVELS = (1, 2, 4, 8, 16, 32, 64)
HGRN_CHUNKS_PER_GROUP = 2
HGRN_HEADS_PER_STEP = 4
MXU_SUBTILE = 512
FFN_TILE = 512
FFN_DOWN_SUBTILE = 1024
INPROJ_SUBTILE = 1024
VMEM_LIMIT_BYTES = 58 * 1024 * 1024
LOG2E = math.log2(math.e)
MASKED_SCORE = -1e30

BF16 = jnp.bfloat16
F32 = jnp.float32


def _rms_scale(v):
    return lax.rsqrt(jnp.mean(v * v, axis=-1, keepdims=True) + RMS_EPS)


def _sigmoid(v):
    return 0.5 * jnp.tanh(0.5 * v) + 0.5


def _dot(a, b):
    return jnp.dot(a, b, preferred_element_type=F32)


def _block_index(offset, width):
    assert offset % width == 0, (offset, width)
    return offset // width


def _dot_nt(a, b):
    return lax.dot_general(a, b, (((1,), (1,)), ((), ())), preferred_element_type=F32)


def _inproj_kernel(x_ref, g_ref, w_ref, o_ref, h_ref):
    @pl.when(pl.program_id(1) == 0)
    def _():
        x = x_ref[...]
        h_ref[...] = (x * _rms_scale(x) * g_ref[...]).astype(BF16)

    for c0 in range(0, o_ref.shape[1], INPROJ_SUBTILE):
        cols = slice(c0, min(c0 + INPROJ_SUBTILE, o_ref.shape[1]))
        o_ref[:, cols] = _dot(h_ref[...], w_ref[:, cols]).astype(o_ref.dtype)


def _inproj(x2, g, w, tm=1024, tn=2432):
    t, d = x2.shape
    n = w.shape[1]
    assert t % tm == 0 and n % tn == 0
    return pl.pallas_call(
        _inproj_kernel,
        grid=(t // tm, n // tn),
        in_specs=[pl.BlockSpec((tm, d), lambda i, j: (i, 0)),
                  pl.BlockSpec((1, d), lambda i, j: (0, 0)),
                  pl.BlockSpec((d, tn), lambda i, j: (0, j))],
        out_specs=pl.BlockSpec((tm, tn), lambda i, j: (i, j)),
        out_shape=jax.ShapeDtypeStruct((t, n), BF16),
        scratch_shapes=[pltpu.VMEM((tm, d), BF16)],
        compiler_params=pltpu.CompilerParams(
            dimension_semantics=("arbitrary", "arbitrary"),
            vmem_limit_bytes=VMEM_LIMIT_BYTES),
        name="in_proj",
    )(x2, g, w)


def _attn_kernel(sink_ref, pos_ref, invf_ref, ctile_ref, stile_ref, q_ref, k_ref, v_ref, *rest,
                 tile_groups):
    n_w = (len(rest) - 3) // 2
    o_ref, (kz_ref, vz_ref) = rest[n_w], rest[-2:]

    def narrow_weights():
        for w_ref, w16_ref, groups in zip(rest[:n_w], rest[n_w + 1:2 * n_w + 1], tile_groups):
            if groups is None:
                w16_ref[...] = w_ref[...].astype(w16_ref.dtype)
                continue
            tiles = w16_ref.shape[0]
            width = w16_ref.shape[2] // groups
            for tile in range(tiles):
                for g in range(groups):
                    src = (g * tiles + tile) * width
                    w16_ref[tile, :, g * width:(g + 1) * width] = (
                        w_ref[:, src:src + width].astype(w16_ref.dtype))

    blk = ATTN_BLOCK
    nblk = q_ref.shape[0] // blk
    n = pl.program_id(1)
    carry_in = nblk + n % 2
    carry_out = nblk + (n + 1) % 2

    @pl.when(n == 0)
    def _():
        kz_ref[nblk] = jnp.zeros(kz_ref.shape[1:], kz_ref.dtype)
        vz_ref[nblk] = jnp.zeros(vz_ref.shape[1:], vz_ref.dtype)

    ang_t = invf_ref[...] * pos_ref[0]
    q_scale = LOG2E * ATTN_HEAD_DIM ** -0.5

    def spread(tab_t, tile_ref):
        p1 = tab_t.astype(BF16)
        r1 = tab_t - p1.astype(F32)
        p2 = r1.astype(BF16)
        p3 = (r1 - p2.astype(F32)).astype(BF16)
        stacked = jnp.concatenate([p1, p2, p3, jnp.zeros_like(p1)], axis=0)
        return _dot(stacked.astype(F32).T.astype(BF16), tile_ref[...])

    cos_t = jnp.cos(ang_t)
    sin_t = jnp.sin(ang_t)
    cos = spread(cos_t, ctile_ref)
    sin_signed = spread(sin_t, stile_ref)
    cos_q = spread(cos_t * q_scale, ctile_ref)
    sin_q = spread(sin_t * q_scale, stile_ref)

    lane = lax.broadcasted_iota(jnp.int32, (blk, LANES), 1)
    row = lax.broadcasted_iota(jnp.int32, (blk, LANES), 0)
    first_half = (lane & (ATTN_HEAD_DIM // 2)) == 0
    low_head = lane < ATTN_HEAD_DIM
    from_prev = lane > row

    def rope(t, j, c, s):
        rows = slice(j * blk, (j + 1) * blk)
        partner = jnp.where(first_half,
                            pltpu.roll(t, LANES - ATTN_HEAD_DIM // 2, 1),
                            pltpu.roll(t, ATTN_HEAD_DIM // 2, 1))
        return t * c[rows] + partner * s[rows]

    def place(src, dst, j, c):
        slots = (j, carry_out) if j == nblk - 1 else (j,)
        for e in range(2):
            own = jnp.where(low_head if e == 0 else jnp.logical_not(low_head), src, 0.0)
            other = pltpu.roll(own, ATTN_HEAD_DIM, 1)
            lo, hi = (own, other) if e == 0 else (other, own)
            for slot in slots:
                dst[slot, 2 * c + e, 0:blk, 0:LANES] = lo.astype(BF16)
                dst[slot, 2 * c + e, blk:2 * blk, 0:LANES] = hi.astype(BF16)

    def place_keys(j):
        rows = slice(j * blk, (j + 1) * blk)
        for c in range(ATTN_KV_WIDTH // LANES):
            place(rope(k_ref[rows, c * LANES:(c + 1) * LANES].astype(F32), j, cos, sin_signed),
                  kz_ref, j, c)

    def place_values(j):
        rows = slice(j * blk, (j + 1) * blk)
        ones_lo = low_head.astype(BF16)
        for slot in ((j, carry_out) if j == nblk - 1 else (j,)):
            for hk in range(ATTN_KV_HEADS):
                vz_ref[slot, hk, 0:blk, LANES:2 * LANES] = ones_lo
                vz_ref[slot, hk, blk:2 * blk, LANES:2 * LANES] = 1 - ones_lo
        for c in range(ATTN_KV_WIDTH // LANES):
            place(v_ref[rows, c * LANES:(c + 1) * LANES].astype(F32), vz_ref, j, c)

    first_penalty = jnp.where(n > 0, 0.0, MASKED_SCORE)
    items = [(j, hk) for j in range(nblk) for hk in range(ATTN_KV_HEADS)]

    def scores(j, hk):
        rows = slice(j * blk, (j + 1) * blk)
        q = jnp.concatenate(
            [rope(q_ref[rows, g * LANES:(g + 1) * LANES].astype(F32), j, cos_q, sin_q).astype(BF16)
             for g in (2 * hk, 2 * hk + 1)], axis=0)
        prv = carry_in if j == 0 else j - 1
        s_prev, s_cur = _dot_nt(q, kz_ref[prv, hk]), _dot_nt(q, kz_ref[j, hk])
        tiles = []
        for rh in range(2):
            rows = slice(rh * blk, (rh + 1) * blk)
            for ch in range(2):
                cols = slice(ch * blk, (ch + 1) * blk)
                sp = s_prev[rows, cols] + first_penalty if j == 0 else s_prev[rows, cols]
                tiles.append(jnp.where(from_prev, sp, s_cur[rows, cols]))
        return (tiles,)

    def fold_and_max(j, hk, tiles):
        folded = []
        for n_tile, s in enumerate(tiles):
            sink = sink_ref[4 * hk + n_tile] * LOG2E
            folded.append((s, jnp.maximum(jnp.max(s, axis=-1, keepdims=True), sink), sink))
        return folded

    def weigh(j, hk, folded):
        ps, sink_terms = [], []
        for s, m, sink in folded:
            ps.append(jnp.exp2(s - m))
            sink_terms.append(jnp.exp2(sink - m))
        stack = lambda tiles: jnp.concatenate(
            [jnp.concatenate(tiles[0:2], axis=1), jnp.concatenate(tiles[2:4], axis=1)], axis=0)
        p_prev = stack([jnp.where(from_prev, p, 0.0).astype(BF16) for p in ps])
        p_cur = stack([jnp.where(from_prev, 0.0, p).astype(BF16) for p in ps])
        prv = carry_in if j == 0 else j - 1
        acc = _dot(p_prev, vz_ref[prv, hk]) + _dot(p_cur, vz_ref[j, hk])
        return acc, sink_terms

    def finish(j, hk, acc, sink_terms):
        for rh in range(2):
            rows = slice(rh * blk, (rh + 1) * blk)
            g = 2 * hk + rh
            denom = acc[rows, LANES:] + jnp.where(low_head, sink_terms[2 * rh],
                                                  sink_terms[2 * rh + 1])
            o_ref[j * blk:(j + 1) * blk, g * LANES:(g + 1) * LANES] = (
                acc[rows, :LANES] / denom).astype(o_ref.dtype)

    place_keys(0)
    s_vals, f_vals, w_vals = {}, {}, {}
    width = ATTN_ITEMS_PER_SLOT
    for slot in range(len(items) // width + 3):
        for k in range((slot - 3) * width, (slot - 2) * width):
            if 0 <= k < len(items):
                finish(*items[k], *w_vals.pop(k))
        for k in range((slot - 2) * width, (slot - 1) * width):
            if 0 <= k < len(items):
                w_vals[k] = weigh(*items[k], f_vals.pop(k))
        for k in range((slot - 1) * width, slot * width):
            if 0 <= k < len(items):
                f_vals[k] = fold_and_max(*items[k], *s_vals.pop(k))
        for k in range(slot * width, (slot + 1) * width):
            if k < len(items):
                j, hk = items[k]
                s_vals[k] = scores(j, hk)
                if hk == 0:
                    place_values(j)
                if hk == 1 and j + 1 < nblk:
                    place_keys(j + 1)
        if slot == len(items) // width // 2:
            narrow_weights()


def _rope_constants():
    half = ATTN_HEAD_DIM // 2
    inv_freq = ROPE_THETA ** (-np.arange(half, dtype=np.float32) / half)
    k = np.arange(LANES)[:, None]
    lane = np.arange(LANES)[None, :]
    hit = ((k < 3 * half) & (k % half == lane % half)).astype(np.float32)
    sign = np.where(lane % ATTN_HEAD_DIM < half, -1.0, 1.0).astype(np.float32)
    return inv_freq.reshape(half, 1), hit, hit * sign


def _attention(proj, positions, sinks, weights, tilings, batch, seq):
    blk = ATTN_BLOCK
    rows = ATTN_BLOCKS_PER_STEP * blk
    assert seq % rows == 0
    nb = seq // rows
    t = batch * seq
    steps = batch * nb
    w_specs, w16_specs, w16_shapes = [], [], []
    for w, tiling in zip(weights, tilings):
        assert w.shape[0] % (steps * 2 * SUBLANES) == 0, w.shape
        blk_rows = w.shape[0] // steps
        w_specs.append(pl.BlockSpec((blk_rows, w.shape[1]), lambda b, n: (b * nb + n, 0)))
        if tiling is None:
            w16_specs.append(w_specs[-1])
            w16_shapes.append(jax.ShapeDtypeStruct(w.shape, BF16))
        else:
            width = tiling[0] * tiling[1]
            tiles = w.shape[1] // width
            w16_specs.append(pl.BlockSpec((tiles, blk_rows, width), lambda b, n: (0, b * nb + n, 0)))
            w16_shapes.append(jax.ShapeDtypeStruct((tiles, w.shape[0], width), BF16))
    tile_groups = tuple(None if tiling is None else tiling[1] for tiling in tilings)
    inv_freq, ctile, stile = _rope_constants()
    pos_rows = positions.astype(F32).reshape(t // rows, 1, rows)
    const = lambda a: pl.BlockSpec(a.shape, lambda b, n: (0, 0))
    q_blk, k_blk, v_blk = (_block_index(PROJ_OFFSETS[0], ATTN_Q_WIDTH),
                           _block_index(PROJ_OFFSETS[1], ATTN_KV_WIDTH),
                           _block_index(PROJ_OFFSETS[2], ATTN_KV_WIDTH))
    slots = ATTN_BLOCKS_PER_STEP + 2
    k_scratch = pltpu.VMEM((slots, ATTN_KV_HEADS, 2 * blk, LANES), BF16)
    v_scratch = pltpu.VMEM((slots, ATTN_KV_HEADS, 2 * blk, 2 * LANES), BF16)
    outs = pl.pallas_call(
        functools.partial(_attn_kernel, tile_groups=tile_groups),
        grid=(batch, nb),
        in_specs=[pl.BlockSpec(memory_space=pltpu.SMEM),
                  pl.BlockSpec((1, 1, rows), lambda b, n: (b * nb + n, 0, 0)),
                  const(inv_freq), const(ctile), const(stile),
                  pl.BlockSpec((rows, ATTN_Q_WIDTH), lambda b, n: (b * nb + n, q_blk)),
                  pl.BlockSpec((rows, ATTN_KV_WIDTH), lambda b, n: (b * nb + n, k_blk)),
                  pl.BlockSpec((rows, ATTN_KV_WIDTH), lambda b, n: (b * nb + n, v_blk))] + w_specs,
        out_specs=[pl.BlockSpec((rows, ATTN_Q_WIDTH), lambda b, n: (b * nb + n, 0))] + w16_specs,
        out_shape=[jax.ShapeDtypeStruct((t, ATTN_Q_WIDTH), BF16)] + w16_shapes,
        scratch_shapes=[k_scratch, v_scratch],
        compiler_params=pltpu.CompilerParams(
            dimension_semantics=("arbitrary", "arbitrary"),
            vmem_limit_bytes=VMEM_LIMIT_BYTES),
        name="swa_attention",
    )(sinks, pos_rows, jnp.asarray(inv_freq), jnp.asarray(ctile, dtype=BF16),
      jnp.asarray(stile, dtype=BF16), proj, proj, proj, *weights)
    return outs[0], outs[1:]


def _hgrn_constants():
    c = HGRN_CHUNK
    u = np.arange(c)[None, :]
    t = np.arange(c)[:, None]
    ranges, masks = [], []
    for m in HGRN_LEVELS:
        upper = (t & m) != 0
        start = (t // m) * m
        q_rng = (u >= start) & (u <= t)
        k_rng = (u > t) & (u <= start + m - 1)
        if m < SUBLANES:
            ranges.append(np.where(upper, q_rng, k_rng))
        masks.append(upper & ((u & m) == 0) & ((t // (2 * m)) == (u // (2 * m))))
    ranges.append(u <= t)
    masks.append(u == t)
    rng = np.concatenate(ranges, axis=0).astype(np.float32)
    return np.concatenate([rng, rng], axis=1), np.concatenate(masks, axis=0).astype(np.float32)


def _hgrn_kernel(lbl_ref, gn_ref, w_ref, m_ref, q_ref, f_ref, i_ref, g_ref, o_ref, st_ref,
                 *, layer):
    c = HGRN_CHUNK
    nl = len(HGRN_LEVELS)
    n_fine = sum(m < SUBLANES for m in HGRN_LEVELS)
    heads = q_ref.shape[1] // HGRN_DK
    n_chunks = q_ref.shape[0] // c

    @pl.when(pl.program_id(2) == 0)
    def _():
        st_ref[...] = jnp.zeros_like(st_ref)

    lg = lbl_ref[...]
    eg = jnp.exp(lg - jnp.max(lg, axis=0, keepdims=True))
    lb_all = (jnp.sum(eg[0:layer + 1, :], axis=0, keepdims=True)
              / jnp.sum(eg, axis=0, keepdims=True))
    gn = gn_ref[...]

    head_cols = [slice(h * HGRN_DK, (h + 1) * HGRN_DK) for h in range(heads)]

    def gates(ci):
        rows = slice(ci * c, (ci + 1) * c)
        qts, kks, pieces = [], [], []
        for cols in head_cols:
            f_half = 0.5 * (1.0 - lb_all[:, cols])
            f_mid = lb_all[:, cols] + f_half
            z = f_ref[rows, cols].astype(F32)
            qr = q_ref[rows, cols].astype(F32)
            ft = f_half * jnp.tanh(0.5 * z)
            logf = jnp.log2(f_mid + ft)
            kks.append(f_half - ft)
            qh = (0.5 * HGRN_DK ** -0.5) * qr
            qts.append(qh + qh * jnp.tanh(0.5 * qr))
            hi = logf.astype(BF16)
            pieces.append((hi, (logf - hi.astype(F32)).astype(BF16)))
        split = jnp.concatenate(
            [jnp.concatenate([p[0] for p in pieces], axis=1),
             jnp.concatenate([p[1] for p in pieces], axis=1)], axis=0)
        return qts, kks, _dot(w_ref[...], split)

    def coarse_exponent(b, m):
        pieces = []
        for lo in range(0, c, 2 * m):
            ref = b[lo + m - 1:lo + m, :]
            pieces += [ref - b[lo:lo + m, :], b[lo + m:lo + 2 * m, :] - ref]
        return jnp.concatenate(pieces, axis=0)

    def intra(group):
        chains = [(ci, cols, qts[h].astype(BF16), kks[h].astype(BF16), xs,
                   xs[n_fine * c:(n_fine + 1) * c, cols])
                  for ci, qts, kks, xs in group for h, cols in enumerate(head_cols)]
        accs = [m_ref[nl * c:(nl + 1) * c, :] * _dot_nt(q16, k16)
                for _, _, q16, k16, _, _ in chains]
        for li, m in enumerate(HGRN_LEVELS):
            mask = m_ref[li * c:(li + 1) * c, :]
            for n, (_, cols, q16, k16, xs, b) in enumerate(chains):
                x = xs[li * c:(li + 1) * c, cols] if li < n_fine else coarse_exponent(b, m)
                e = jnp.exp2(x).astype(BF16)
                accs[n] = accs[n] + mask * _dot_nt(q16 * e, k16 * e)
        parts = {ci: [] for ci, _, _, _ in group}
        for (ci, cols, q16, k16, _, b), acc in zip(chains, accs):
            eb = jnp.exp2(b)
            el = jnp.exp2(b[c - 1:c, :] - b)
            v = i_ref[ci * c:(ci + 1) * c, cols]
            parts[ci].append((q16 * eb.astype(BF16), _dot(acc.astype(BF16), v), eb[c - 1:c, :],
                              _dot(v.astype(F32).T.astype(BF16), k16 * el.astype(BF16))))
        return parts

    def carry_state(ci, parts, states):
        rows = slice(ci * c, (ci + 1) * c)
        new_states = []
        for h, cols in enumerate(head_cols):
            q_decayed, o_intra, chunk_decay, kv = parts[h]
            st = states[h]
            o = _dot_nt(q_decayed, st.astype(BF16)) + o_intra
            new_states.append(st * chunk_decay + kv)
            gate = g_ref[rows, cols].astype(F32)
            y = o * _rms_scale(o) * gn
            gh = 0.5 * gate
            y = y * (gh + gh * jnp.tanh(gh))
            o_ref[rows, cols] = y.astype(o_ref.dtype)
        return new_states

    states = [st_ref[h] for h in range(heads)]
    gs = HGRN_CHUNKS_PER_GROUP
    groups = [list(range(g0, min(g0 + gs, n_chunks))) for g0 in range(0, n_chunks, gs)]
    g_vals = {0: [(ci,) + gates(ci) for ci in groups[0]]}
    i_vals = {}
    for k in range(len(groups) + 1):
        if k >= 1:
            parts = i_vals.pop(k - 1)
            for ci in groups[k - 1]:
                states = carry_state(ci, parts[ci], states)
        if k < len(groups):
            i_vals[k] = intra(g_vals.pop(k))
        if k + 1 < len(groups):
            g_vals[k + 1] = [(ci,) + gates(ci) for ci in groups[k + 1]]
    for h in range(heads):
        st_ref[h] = states[h]


def _hgrn(proj, lb_logits, gnorm, layer, batch, seq, rows=1024):
    t = batch * seq
    steps = seq // rows
    hp = HGRN_HEADS_PER_STEP
    width = hp * HGRN_DK
    q_blk, f_blk, i_blk, g_blk = (_block_index(PROJ_OFFSETS[s], width) for s in (3, 4, 5, 6))
    rng, masks = _hgrn_constants()
    rng = jnp.asarray(rng, dtype=BF16)
    masks = jnp.asarray(masks, dtype=F32)
    const = lambda a: pl.BlockSpec(a.shape, lambda b, h, s: (0, 0))
    row_spec = lambda base: pl.BlockSpec(
        (rows, width), lambda b, h, s, base=base: (b * steps + s, base + h))
    return pl.pallas_call(
        functools.partial(_hgrn_kernel, layer=layer),
        grid=(batch, HGRN_HEADS // hp, steps),
        in_specs=[pl.BlockSpec((lb_logits.shape[0], width), lambda b, h, s: (0, h)),
                  pl.BlockSpec((1, HGRN_DV), lambda b, h, s: (0, 0)),
                  const(rng), const(masks),
                  row_spec(q_blk), row_spec(f_blk), row_spec(i_blk), row_spec(g_blk)],
        out_specs=pl.BlockSpec((rows, width), lambda b, h, s: (b * steps + s, h)),
        out_shape=jax.ShapeDtypeStruct((t, HGRN_WIDTH), BF16),
        scratch_shapes=[pltpu.VMEM((hp, HGRN_DV, HGRN_DK), F32)],
        compiler_params=pltpu.CompilerParams(
            dimension_semantics=("arbitrary", "arbitrary", "arbitrary")),
        name="hgrn2",
    )(lb_logits, gnorm, rng, masks, proj, proj, proj, proj)


def _merge_kernel(x_ref, ya_ref, yh_ref, wa_ref, wh_ref, wo_ref, gp_ref, *rest):
    n_tiles = (len(rest) - 2) // 2
    ga_refs, gb_refs = rest[:n_tiles], rest[n_tiles:2 * n_tiles]
    o_ref, merged_ref = rest[2 * n_tiles:]
    ya, yh = ya_ref[...], yh_ref[...]
    tile_cols = [slice(c * MXU_SUBTILE, (c + 1) * MXU_SUBTILE) for c in range(n_tiles)]
    for cols, ga_ref, gb_ref in zip(tile_cols, ga_refs, gb_refs):
        merged_ref[:, cols] = (
            _sigmoid(ga_ref[...].astype(F32)) * _dot(ya, wa_ref[:, cols])
            + _sigmoid(gb_ref[...].astype(F32)) * _dot(yh, wh_ref[:, cols])).astype(BF16)
    merged = merged_ref[...]
    for cols in tile_cols:
        o_ref[:, cols] = _dot(merged, wo_ref[:, cols])
    m2 = o_ref[...]
    o_ref[...] = x_ref[...] + m2 * _rms_scale(m2) * gp_ref[...]


def _merge(x2, y_attn, y_hgrn, proj, wa, wh, wo, g_post, tm=512):
    t, d = x2.shape
    n_tiles = d // MXU_SUBTILE
    ga_blk, gb_blk = (_block_index(PROJ_OFFSETS[s], MXU_SUBTILE) for s in (7, 8))
    const = lambda shape: pl.BlockSpec(shape, lambda i: (0, 0), pipeline_mode=pl.Buffered(1))
    gate_specs = [pl.BlockSpec((tm, MXU_SUBTILE), lambda i, blk=base + c: (i, blk))
                  for base in (ga_blk, gb_blk) for c in range(n_tiles)]
    return pl.pallas_call(
        _merge_kernel,
        grid=(t // tm,),
        in_specs=[pl.BlockSpec((tm, d), lambda i: (i, 0)),
                  pl.BlockSpec((tm, ATTN_Q_WIDTH), lambda i: (i, 0)),
                  pl.BlockSpec((tm, HGRN_WIDTH), lambda i: (i, 0)),
                  const(wa.shape), const(wh.shape), const(wo.shape), const((1, d))] + gate_specs,
        out_specs=pl.BlockSpec((tm, d), lambda i: (i, 0)),
        out_shape=jax.ShapeDtypeStruct((t, d), F32),
        scratch_shapes=[pltpu.VMEM((tm, d), BF16)],
        compiler_params=pltpu.CompilerParams(
            dimension_semantics=("arbitrary",), vmem_limit_bytes=VMEM_LIMIT_BYTES),
        name="merge_out",
    )(x2, y_attn, y_hgrn, wa, wh, wo, g_post, *([proj] * (2 * n_tiles)))


def _ffn_kernel(x_ref, gpre_ref, wgu_hbm, wd_hbm, gpost_ref, o_ref, h_ref, wgu_buf, wd_buf, sem):
    i = pl.program_id(0)
    tf = wd_buf.shape[1]
    n_tiles = D_FF // tf
    ring = wgu_buf.shape[0] - 1
    slot_of = lambda j: ring if j == 0 else (j - 1) % ring

    def weight_copies(j):
        slot = slot_of(j)
        return (pltpu.make_async_copy(wgu_hbm.at[j], wgu_buf.at[slot], sem.at[0, slot]),
                pltpu.make_async_copy(wd_hbm.at[pl.ds(j * tf, tf), :], wd_buf.at[slot],
                                      sem.at[1, slot]))

    def start_copies(j):
        for priority, copy in enumerate(weight_copies(j)):
            copy.start(priority=priority)

    @pl.when(i == 0)
    def _():
        start_copies(0)

    x = x_ref[...]
    h_ref[...] = (x * _rms_scale(x) * gpre_ref[...]).astype(BF16)
    h = h_ref[...]
    ahead = ring - 1
    for t in range(1, ahead):
        start_copies(t)
    for j in range(n_tiles):
        for copy in weight_copies(j):
            copy.wait()
        slot = slot_of(j)
        gate_up = _dot(h, wgu_buf[slot])
        gate, up = gate_up[:, :tf], gate_up[:, tf:]
        if j + ahead < n_tiles:
            start_copies(j + ahead)
        if j == 1:
            start_copies(0)
        act = (gate * _sigmoid(gate) * up).astype(BF16)
        for c0 in range(0, o_ref.shape[1], FFN_DOWN_SUBTILE):
            cols = slice(c0, c0 + FFN_DOWN_SUBTILE)
            down = _dot(act, wd_buf[slot, :, cols])
            o_ref[:, cols] = down if j == 0 else o_ref[:, cols] + down
    y = o_ref[...]
    o_ref[...] = x_ref[...] + y * _rms_scale(y) * gpost_ref[...]

    @pl.when(i == pl.num_programs(0) - 1)
    def _():
        for copy in weight_copies(0):
            copy.wait()


def _ffn(x1, g_pre, w_gate_up, w_down, g_post, tm=512):
    t, d = x1.shape
    tf = FFN_TILE
    assert w_gate_up.shape == (D_FF // tf, d, 2 * tf)
    slots = 5
    return pl.pallas_call(
        _ffn_kernel,
        grid=(t // tm,),
        in_specs=[pl.BlockSpec((tm, d), lambda i: (i, 0)),
                  pl.BlockSpec((1, d), lambda i: (0, 0)),
                  pl.BlockSpec(memory_space=pl.ANY),
                  pl.BlockSpec(memory_space=pl.ANY),
                  pl.BlockSpec((1, d), lambda i: (0, 0))],
        out_specs=pl.BlockSpec((tm, d), lambda i: (i, 0)),
        out_shape=jax.ShapeDtypeStruct((t, d), F32),
        scratch_shapes=[pltpu.VMEM((tm, d), BF16),
                        pltpu.VMEM((slots, d, 2 * tf), BF16), pltpu.VMEM((slots, tf, d), BF16),
                        pltpu.SemaphoreType.DMA((2, slots))],
        compiler_params=pltpu.CompilerParams(
            dimension_semantics=("arbitrary",), vmem_limit_bytes=VMEM_LIMIT_BYTES),
        name="swiglu_ffn",
    )(x1, g_pre, w_gate_up, w_down, g_post)


def _ple_kernel(x_ref, p_ref, gpre_ref, wg_ref, wp_ref, gpost_ref, o_ref):
    x = x_ref[...]
    h = (x * _rms_scale(x) * gpre_ref[...]).astype(BF16)
    p16 = p_ref[...].astype(BF16)
    for c0 in range(0, o_ref.shape[1], MXU_SUBTILE):
        cols = slice(c0, c0 + MXU_SUBTILE)
        gate = _sigmoid(_dot(h, wg_ref[:, cols]))
        o_ref[:, cols] = _dot(p16, wp_ref[:, cols]) * gate
    e = o_ref[...]
    o_ref[...] = x_ref[...] + e * _rms_scale(e) * gpost_ref[...]


def _ple(x2, p2, g_pre, w_gate, w_proj, g_post, tm=512):
    t, d = x2.shape
    const = lambda shape: pl.BlockSpec(shape, lambda i: (0, 0), pipeline_mode=pl.Buffered(1))
    return pl.pallas_call(
        _ple_kernel,
        grid=(t // tm,),
        in_specs=[pl.BlockSpec((tm, d), lambda i: (i, 0)),
                  pl.BlockSpec((tm, PLE_DIM), lambda i: (i, 0)),
                  const((1, d)), const(w_gate.shape), const(w_proj.shape), const((1, d))],
        out_specs=pl.BlockSpec((tm, d), lambda i: (i, 0)),
        out_shape=jax.ShapeDtypeStruct((t, d), F32),
        compiler_params=pltpu.CompilerParams(
            dimension_semantics=("arbitrary",), vmem_limit_bytes=VMEM_LIMIT_BYTES),
        name="ple",
    )(x2, p2, g_pre, w_gate, w_proj, g_post)


def kernel(x, p, positions, g_mix_pre, w_in, attn_sinks, hgrn_lb_logits, hgrn_gnorm,
           w_attn_branch, w_hgrn_branch, w_out, g_mix_post, g_ffn_pre, w_gate_up, w_down,
           g_ffn_post, g_ple_pre, w_ple_gate, w_ple_proj, g_ple_post):
    batch, seq, d = x.shape
    t = batch * seq
    depth = w_in.shape[0]
    row = lambda v: v.reshape(1, -1)

    xs = x.reshape(t, d)
    for layer in range(depth):
        proj = _inproj(xs, row(g_mix_pre[layer]), w_in[layer].astype(BF16))
        y_attn, (wa, wh, wo, wgu, wd, wpg) = _attention(
            proj, positions, attn_sinks[layer],
            (w_attn_branch[layer], w_hgrn_branch[layer], w_out[layer], w_gate_up[layer],
             w_down[layer], w_ple_gate[layer]),
            (None, None, None, (FFN_TILE, 2), None, None), batch, seq)
        y_hgrn = _hgrn(proj, hgrn_lb_logits, row(hgrn_gnorm[layer]), layer, batch, seq)
        xs = _merge(xs, y_attn, y_hgrn, proj, wa, wh, wo, row(g_mix_post[layer]))
        xs = _ffn(xs, row(g_ffn_pre[layer]), wgu, wd, row(g_ffn_post[layer]))
        xs = _ple(xs, p[layer].reshape(t, PLE_DIM), row(g_ple_pre[layer]), wpg,
                  w_ple_proj[layer].astype(BF16), row(g_ple_post[layer]))
    return xs.reshape(batch, seq, d)
```
